```python
import math
import jax, jax.numpy as jnp
from jax import lax
import numpy as np

D_MODEL = 1024
BATCH = 8
SEQ = 4096
DEPTH = 1

CTX_LEN = 256
GRID_W = 64
HA = 8
DA = 64
HB = 8
NOPE = 128
ROPE = 64
VB = 128
Q_LORA = 256
KV_LORA = 256
N_KEYS = 128
N_EXPERTS = N_KEYS * N_KEYS
HP = 8
DK_HALF = 128
TOPK_HALF = 16
TOPK = 16
ROPE_BASE = 10000.0
Q_BLOCK = 128
TOK_BLOCK = 128
EPS = 1e-6
DEEPNORM_ALPHA = (2 * DEPTH) ** 0.25
DEEPNORM_BETA = (8 * DEPTH) ** -0.25
Q_A = HA * 2 * DA
GATES = 2 * D_MODEL
K_A = HA * 2 * DA
V_A = HA * 2 * DA
Q_COLS = Q_A + Q_LORA + GATES
KV_COLS = K_A + V_A + KV_LORA + ROPE
IN_COLS = Q_COLS + KV_COLS

kernel_name = "hybrid_diffattn_mla_peer_block"


def rms_norm(x, g):
    xf = x.astype(jnp.float32)
    y = xf * lax.rsqrt(jnp.mean(xf * xf, -1, keepdims=True) + EPS)
    return (y * g).astype(x.dtype)


def layer_norm(x, g, b):
    xf = x.astype(jnp.float32)
    mu = jnp.mean(xf, -1, keepdims=True)
    var = jnp.mean(jnp.square(xf - mu), -1, keepdims=True)
    return ((xf - mu) * lax.rsqrt(var + EPS) * g + b).astype(x.dtype)


def axial_rope(n, rot_dim):
    rows = n // GRID_W
    quarter = rot_dim // 4
    inv = ROPE_BASE ** (-jnp.arange(quarter, dtype=jnp.float32) / quarter)
    row_ang = jnp.arange(rows, dtype=jnp.float32)[:, None] * inv
    col_ang = jnp.arange(GRID_W, dtype=jnp.float32)[:, None] * inv
    ang = jnp.concatenate([
        jnp.broadcast_to(row_ang[:, None, :], (rows, GRID_W, quarter)),
        jnp.broadcast_to(col_ang[None, :, :], (rows, GRID_W, quarter))], -1).reshape(n, 2 * quarter)
    return jnp.cos(ang), jnp.sin(ang)


def apply_rope(x, cos, sin):
    half = x.shape[-1] // 2
    cos, sin = cos.astype(x.dtype), sin.astype(x.dtype)
    x1, x2 = x[..., :half], x[..., half:]
    return jnp.concatenate([x1 * cos - x2 * sin, x1 * sin + x2 * cos], -1)


def kv_side(h, w_in, w_kv_up, kv_norm_g, rope):
    B, N, _ = h.shape
    p = h @ w_in[:, Q_COLS:]
    k_a, v_a, kv_lat, k_rope = jnp.split(p, [K_A, K_A + V_A, K_A + V_A + KV_LORA], axis=-1)
    k_a = k_a.reshape(B, N, HA, 2, DA)
    v_a = v_a.reshape(B, N, HA, 2 * DA)
    kv = (rms_norm(kv_lat, kv_norm_g) @ w_kv_up).reshape(B, N, HB, NOPE + VB)
    k_nope, v_b = kv[..., :NOPE], kv[..., NOPE:]
    if rope is not None:
        cos, sin = rope
        k_a = apply_rope(k_a, cos[:, None, None, :], sin[:, None, None, :])
        k_rope = apply_rope(k_rope, cos, sin)
    return (k_a, v_a, k_nope, v_b, k_rope)


def q_side(h, w_in, w_q_up, q_norm_g, rope):
    B, N, _ = h.shape
    p = h @ w_in[:, :Q_COLS]
    q_a, q_lat, gates = jnp.split(p, [Q_A, Q_A + Q_LORA], axis=-1)
    q_a = q_a.reshape(B, N, HA, 2, DA)
    q_b = (rms_norm(q_lat, q_norm_g) @ w_q_up).reshape(B, N, HB, NOPE + ROPE)
    q_nope, q_rope = q_b[..., :NOPE], q_b[..., NOPE:]
    if rope is not None:
        cos, sin = rope
        q_a = apply_rope(q_a, cos[:, None, None, :], sin[:, None, None, :])
        q_rope = apply_rope(q_rope, cos[:, None, :], sin[:, None, :])
    return (q_a, q_nope, q_rope, gates)


def sweep(fn, *qs):
    B, N = qs[0].shape[:2]
    nb = N // Q_BLOCK
    blk = tuple(jnp.moveaxis(q.reshape(B, nb, Q_BLOCK, *q.shape[2:]), 1, 0) for q in qs)
    out = lax.map(lambda a: fn(*a), blk)
    return jnp.moveaxis(out, 0, 1).reshape(B, N, *out.shape[3:])


def diff_attention(q_a, k_a, v_a, lam, subln_g, lambda_init):
    scale = DA ** -0.5

    def block(qb):
        s = jnp.einsum('bqhmd,bkhmd->bhmqk', qb, k_a).astype(jnp.float32) * scale
        p = jax.nn.softmax(s, axis=-1)
        p = p[:, :, 0] - lam * p[:, :, 1]
        return jnp.einsum('bhqk,bkhe->bqhe', p.astype(v_a.dtype), v_a)

    o = sweep(block, q_a)
    return rms_norm(o, subln_g) * (1.0 - lambda_init)


def mla_attention(q_nope, q_rope, k_nope, k_rope, v_b):
    scale = (NOPE + ROPE) ** -0.5

    def block(qn, qr):
        s = (jnp.einsum('bqhd,bkhd->bhqk', qn, k_nope)
             + jnp.einsum('bqhr,bkr->bhqk', qr, k_rope)).astype(jnp.float32) * scale
        p = jax.nn.softmax(s, axis=-1)
        return jnp.einsum('bhqk,bkhe->bqhe', p.astype(v_b.dtype), v_b)

    return sweep(block, q_nope, q_rope)


def token_mixers(q, kv, lam, lambda_init, subln_g, w_pa, w_pb, w_out):
    q_a, q_nope, q_rope, gates = q
    k_a, v_a, k_nope, v_b, k_rope = kv
    B, N = gates.shape[:2]
    y_a = diff_attention(q_a, k_a, v_a, lam, subln_g, lambda_init).reshape(B, N, HA * 2 * DA)
    y_b = mla_attention(q_nope, q_rope, k_nope, k_rope, v_b).reshape(B, N, HB * VB)
    g_a = jax.nn.sigmoid(gates[..., :D_MODEL])
    g_b = jax.nn.sigmoid(gates[..., D_MODEL:])
    z = g_a * (y_a @ w_pa) + g_b * (y_b @ w_pb)
    return z @ w_out


def peer_ffn(h, w_pq, peer_keys, peer_u, peer_v):
    B, N, D = h.shape
    q = (h @ w_pq).reshape(B, N, HP, 2, DK_HALF)
    s = jnp.einsum('bnhmd,hmkd->bnhmk', q, peer_keys).astype(jnp.float32)
    s_top, i_top = lax.top_k(s, TOPK_HALF)
    cand = s_top[..., 0, :, None] + s_top[..., 1, None, :]
    cand_idx = i_top[..., 0, :, None] * N_KEYS + i_top[..., 1, None, :]
    g_s, pos = lax.top_k(cand.reshape(B, N, HP, TOPK_HALF * TOPK_HALF), TOPK)
    idx = jnp.take_along_axis(cand_idx.reshape(B, N, HP, TOPK_HALF * TOPK_HALF), pos, axis=-1)
    g = jax.nn.softmax(g_s, axis=-1)
    T = B * N
    nb = T // TOK_BLOCK
    xt = h.reshape(nb, TOK_BLOCK, D)
    it = idx.reshape(nb, TOK_BLOCK, HP * TOPK)
    gt = g.reshape(nb, TOK_BLOCK, HP * TOPK).astype(h.dtype)

    def block(args):
        xb, ib, gb = args
        u = peer_u[ib]
        v = peer_v[ib]
        a = jnp.einsum('td,tjd->tj', xb, u)
        return jnp.einsum('tj,tjd->td', jax.nn.gelu(a) * gb, v)

    return lax.map(block, (xt, it, gt)).reshape(B, N, D)


def setup_inputs(seed: int = 0) -> dict:
    key = jax.random.key(seed)
    ks = jax.random.split(key, 32)
    L = DEPTH

    def nrm(k, shape, scale):
        return jax.random.normal(k, shape, jnp.float32) * scale

    beta = DEEPNORM_BETA
    return {
        "x": nrm(ks[0], (BATCH, SEQ, D_MODEL), 1.0),
        "c": nrm(ks[1], (BATCH, D_MODEL), 1.0),
        "ctx": nrm(ks[2], (BATCH, CTX_LEN, D_MODEL), 1.0),
        "c_ctx": nrm(ks[3], (D_MODEL,), 1.0),
        "w_mod": nrm(ks[4], (L, D_MODEL, 6 * D_MODEL), 0.5 * D_MODEL ** -0.5),
        "b_mod": nrm(ks[5], (L, 6 * D_MODEL), 0.01),
        "w_in": nrm(ks[6], (L, D_MODEL, IN_COLS), D_MODEL ** -0.5),
        "w_q_up": nrm(ks[7], (L, Q_LORA, HB * (NOPE + ROPE)), Q_LORA ** -0.5),
        "q_norm_g": 1.0 + nrm(ks[8], (L, Q_LORA), 0.02),
        "w_kv_up": nrm(ks[9], (L, KV_LORA, HB * (NOPE + VB)), KV_LORA ** -0.5),
        "kv_norm_g": 1.0 + nrm(ks[10], (L, KV_LORA), 0.02),
        "lambda_q1": nrm(ks[11], (L, DA), 0.1),
        "lambda_k1": nrm(ks[12], (L, DA), 0.1),
        "lambda_q2": nrm(ks[13], (L, DA), 0.1),
        "lambda_k2": nrm(ks[14], (L, DA), 0.1),
        "subln_g": 1.0 + nrm(ks[15], (L, 2 * DA), 0.02),
        "w_pa": nrm(ks[16], (L, HA * 2 * DA, D_MODEL), beta * (HA * 2 * DA) ** -0.5),
        "w_pb": nrm(ks[17], (L, HB * VB, D_MODEL), beta * (HB * VB) ** -0.5),
        "w_out": nrm(ks[18], (L, D_MODEL, D_MODEL), beta * D_MODEL ** -0.5),
        "ln1_g": 1.0 + nrm(ks[19], (L, D_MODEL), 0.02),
        "ln1_b": nrm(ks[20], (L, D_MODEL), 0.01),
        "w_pq": nrm(ks[21], (L, D_MODEL, HP * 2 * DK_HALF), D_MODEL ** -0.5),
        "peer_keys": nrm(ks[22], (L, HP, 2, N_KEYS, DK_HALF), DK_HALF ** -0.5),
        "peer_u": nrm(ks[23], (L, N_EXPERTS, D_MODEL), D_MODEL ** -0.5),
        "peer_v": nrm(ks[24], (L, N_EXPERTS, D_MODEL), beta),
        "ln2_g": 1.0 + nrm(ks[25], (L, D_MODEL), 0.02),
        "ln2_b": nrm(ks[26], (L, D_MODEL), 0.01),
    }


def reference(x, c, ctx, c_ctx, w_mod, b_mod, w_in, w_q_up, q_norm_g, w_kv_up, kv_norm_g,
              lambda_q1, lambda_k1, lambda_q2, lambda_k2, subln_g, w_pa, w_pb, w_out,
              ln1_g, ln1_b, w_pq, peer_keys, peer_u, peer_v, ln2_g, ln2_b):
    rope = axial_rope(x.shape[1], DA)
    alpha = DEEPNORM_ALPHA
    for l in range(DEPTH):
        last = l == DEPTH - 1
        lambda_init = 0.8 - 0.6 * math.exp(-0.3 * l)
        mod = jax.nn.silu(c) @ w_mod[l] + b_mod[l]
        sh1, sc1, gt1, sh2, sc2, gt2 = [m[:, None, :] for m in jnp.split(mod, 6, axis=-1)]
        mod_c = jax.nn.silu(c_ctx) @ w_mod[l] + b_mod[l]
        csh1, csc1, cgt1, csh2, csc2, cgt2 = jnp.split(mod_c, 6, axis=-1)
        lam = (jnp.exp(jnp.sum(lambda_q1[l] * lambda_k1[l]).astype(jnp.float32))
               - jnp.exp(jnp.sum(lambda_q2[l] * lambda_k2[l]).astype(jnp.float32)) + lambda_init)

        h = x * (1.0 + sc1) + sh1
        hc = ctx * (1.0 + csc1) + csh1
        kv_c = kv_side(hc, w_in[l], w_kv_up[l], kv_norm_g[l], None)
        kv_x = kv_side(h, w_in[l], w_kv_up[l], kv_norm_g[l], rope)
        kv_all = tuple(jnp.concatenate([a, b], axis=1) for a, b in zip(kv_c, kv_x))
        q_x = q_side(h, w_in[l], w_q_up[l], q_norm_g[l], rope)
        y = token_mixers(q_x, kv_all, lam, lambda_init, subln_g[l], w_pa[l], w_pb[l], w_out[l])
        x = layer_norm(alpha * x + gt1 * y, ln1_g[l], ln1_b[l])
        if not last:
            q_c = q_side(hc, w_in[l], w_q_up[l], q_norm_g[l], None)
            yc = token_mixers(q_c, kv_c, lam, lambda_init, subln_g[l], w_pa[l], w_pb[l], w_out[l])
            ctx = layer_norm(alpha * ctx + cgt1 * yc, ln1_g[l], ln1_b[l])

        h = x * (1.0 + sc2) + sh2
        x = layer_norm(alpha * x + gt2 * peer_ffn(h, w_pq[l], peer_keys[l], peer_u[l], peer_v[l]),
                       ln2_g[l], ln2_b[l])
        if not last:
            hc = ctx * (1.0 + csc2) + csh2
            ctx = layer_norm(alpha * ctx + cgt2 * peer_ffn(hc, w_pq[l], peer_keys[l], peer_u[l], peer_v[l]),
                             ln2_g[l], ln2_b[l])
    return x
```

```python
import functools
import math

import jax
import jax.numpy as jnp
from jax import lax
from jax.experimental import pallas as pl
from jax.experimental.pallas import tpu as pltpu

GRID_W = 64
HA, DA = 8, 64
HB, NOPE, ROPE, VB = 8, 128, 64, 128
Q_LORA, KV_LORA = 256, 256
N_KEYS = 128
HP, DK_HALF = 8, 128
TOPK_HALF, TOPK = 16, 16
ROPE_BASE = 10000.0
EPS = 1e-6
DEPTH = 1
DEEPNORM_ALPHA = (2 * DEPTH) ** 0.25
LAMBDA_INIT = 0.8 - 0.6 * math.exp(-0.3 * 0)

LANES = 128
VMEM_LIMIT_BYTES = 56 * 1024 * 1024

TOKEN_TILE = 256
Q_TILE = 256
ROUTE_TILE = 256
EXPERT_TILE = 128

F32 = jnp.float32
BF16 = jnp.bfloat16


def _cparams(sem):
    return pltpu.CompilerParams(dimension_semantics=sem, vmem_limit_bytes=VMEM_LIMIT_BYTES)


def _full(shape):
    n = len(shape)
    return pl.BlockSpec(shape, lambda *_: (0,) * n)


def _resident(shape):
    n = len(shape)
    return pl.BlockSpec(shape, lambda *_: (0,) * n, pipeline_mode=pl.Buffered(1))


def _mod_kernel(c_ref, w_ref, b_ref, o_ref):
    c = c_ref[...]
    s = c * jax.nn.sigmoid(c)
    o_ref[...] = jnp.dot(s, w_ref[...], preferred_element_type=F32) + b_ref[...]


def _modulation(c_rows, w_mod, b_mod):
    r, d = c_rows.shape
    n = w_mod.shape[1]
    tn = 1536
    return pl.pallas_call(
        _mod_kernel,
        out_shape=jax.ShapeDtypeStruct((r, n), F32),
        grid=(n // tn,),
        in_specs=[_full((r, d)), pl.BlockSpec((d, tn), lambda j: (0, j)), pl.BlockSpec((1, tn), lambda j: (0, j))],
        out_specs=pl.BlockSpec((r, tn), lambda j: (0, j)),
        compiler_params=_cparams(("arbitrary",)),
        name="modulation",
    )(c_rows, w_mod, b_mod)


def _swap_halves(blk):
    lane = lax.broadcasted_iota(jnp.int32, blk.shape, 1)
    up = pltpu.roll(blk, LANES - 32, axis=1)
    dn = pltpu.roll(blk, 32, axis=1)
    return jnp.where((lane % 64) < 32, up, dn)


def _rope_block(blk, c, s):
    return blk * c + _swap_halves(blk) * s


def _rms(x, g):
    return x * lax.rsqrt(jnp.mean(x * x, axis=-1, keepdims=True) + EPS) * g


def _proj_kernel(nct, x_ref, ctx_ref, sc_ref, sh_ref, w_ref, qg_ref, kvg_ref, wq_ref, wkv_ref,
                 cf_ref, sf_ref, ch_ref, shh_ref,
                 qa_ref, qm_ref, gs_ref, ka_ref, va_ref, km_ref, vb_ref):
    i = pl.program_id(1)
    d = x_ref.shape[-1]
    xt = jnp.where(i < nct, ctx_ref[0], x_ref[0])
    h = xt * (1.0 + sc_ref[0]) + sh_ref[0]
    hb = h.astype(BF16)
    cf, sf = cf_ref[...], sf_ref[...]
    ch, shh = ch_ref[...], shh_ref[...]

    def proj(lo, hi):
        return jnp.dot(hb, w_ref[:, lo:hi], preferred_element_type=F32)

    c_qa, c_ql, c_g = 0, HA * 2 * DA, HA * 2 * DA + Q_LORA
    c_ka = c_g + 2 * d
    c_va = c_ka + HA * 2 * DA
    c_kvl = c_va + HA * 2 * DA
    c_kr = c_kvl + KV_LORA

    ka = proj(c_ka, c_va)
    for b in range(ka.shape[1] // LANES):
        sl = slice(b * LANES, (b + 1) * LANES)
        ka_ref[0, :, sl] = _rope_block(ka[:, sl], cf, sf).astype(BF16)
    va_ref[0] = proj(c_va, c_kvl).astype(BF16)
    kvl = _rms(proj(c_kvl, c_kr), kvg_ref[...]).astype(BF16)
    kv = jnp.dot(kvl, wkv_ref[...], preferred_element_type=F32)
    kr = _rope_block(proj(c_kr, c_kr + LANES), ch, shh).astype(BF16)
    for hh in range(HB):
        base = hh * (NOPE + VB)
        km_ref[0, :, hh * 256:hh * 256 + NOPE] = kv[:, base:base + NOPE].astype(BF16)
        km_ref[0, :, hh * 256 + NOPE:(hh + 1) * 256] = kr
        vb_ref[0, :, hh * VB:(hh + 1) * VB] = kv[:, base + NOPE:base + NOPE + VB].astype(BF16)

    @pl.when(i >= nct)
    def _():
        qa = proj(c_qa, c_ql)
        for b in range(qa.shape[1] // LANES):
            sl = slice(b * LANES, (b + 1) * LANES)
            qa_ref[0, :, sl] = (_rope_block(qa[:, sl], cf, sf) * (DA ** -0.5)).astype(BF16)
        ql = _rms(proj(c_ql, c_g), qg_ref[...]).astype(BF16)
        qb = jnp.dot(ql, wq_ref[...], preferred_element_type=F32)
        for hh in range(HB):
            qm_ref[0, :, hh * 256:hh * 256 + NOPE] = qb[:, hh * 256:hh * 256 + NOPE].astype(BF16)
            qm_ref[0, :, hh * 256 + NOPE:(hh + 1) * 256] = _rope_block(
                qb[:, hh * 256 + NOPE:(hh + 1) * 256], ch, shh).astype(BF16)
        gs_ref[0] = jax.nn.sigmoid(proj(c_g, c_ka)).astype(BF16)


def _projection(x, ctx, sc1, sh1, w_all, q_norm_g, kv_norm_g, wq, wkv, cf, sf, ch, shh):
    bsz, n, d = x.shape
    nctx = ctx.shape[1]
    tn = TOKEN_TILE
    nct = nctx // tn
    s = nctx + n
    ncols = w_all.shape[1]

    def lat(b, i):
        return (b, jnp.maximum(i - nct, 0), 0)

    def cx(b, i):
        return (b, jnp.minimum(i, nct - 1), 0)

    def modrow(b, i):
        return (jnp.where(i < nct, bsz, b), 0, 0)

    def allk(b, i):
        return (b, i, 0)

    def tab(b, i):
        return (i, 0)

    bf = lambda w: jax.ShapeDtypeStruct((bsz, n, w), BF16)
    bfs = lambda w: jax.ShapeDtypeStruct((bsz, s, w), BF16)
    return pl.pallas_call(
        functools.partial(_proj_kernel, nct),
        out_shape=(bf(HA * 2 * DA), bf(HB * 256), bf(2 * d), bfs(HA * 2 * DA), bfs(HA * 2 * DA), bfs(HB * 256), bfs(HB * VB)),
        grid=(bsz, s // tn),
        in_specs=[
            pl.BlockSpec((1, tn, d), lat), pl.BlockSpec((1, tn, d), cx),
            pl.BlockSpec((1, 1, d), modrow), pl.BlockSpec((1, 1, d), modrow),
            _resident((d, ncols)), _full((1, Q_LORA)), _full((1, KV_LORA)),
            _resident(wq.shape), _resident(wkv.shape),
            pl.BlockSpec((tn, LANES), tab), pl.BlockSpec((tn, LANES), tab),
            pl.BlockSpec((tn, LANES), tab), pl.BlockSpec((tn, LANES), tab),
        ],
        out_specs=(
            pl.BlockSpec((1, tn, HA * 2 * DA), lat), pl.BlockSpec((1, tn, HB * 256), lat),
            pl.BlockSpec((1, tn, 2 * d), lat),
            pl.BlockSpec((1, tn, HA * 2 * DA), allk), pl.BlockSpec((1, tn, HA * 2 * DA), allk),
            pl.BlockSpec((1, tn, HB * 256), allk), pl.BlockSpec((1, tn, HB * VB), allk),
        ),
        compiler_params=_cparams(("parallel", "arbitrary")),
        name="projection",
    )(x, ctx, sc1, sh1, w_all, q_norm_g, kv_norm_g, wq, wkv, cf, sf, ch, shh)


_NT = (((1,), (1,)), ((), ()))


def _softmax_pv(s, v, scale):
    m = jnp.max(s, axis=-1, keepdims=True)
    p = jnp.exp((s - m) * scale)
    l = jnp.sum(p, axis=-1, keepdims=True)
    o = jnp.dot(p.astype(BF16), v, preferred_element_type=F32)
    return o / l


def _diff_attn_kernel(q_ref, k_ref, v_ref, lam_ref, g_ref, o_ref):
    q = q_ref[0]
    k = k_ref[0]
    v = v_ref[0]
    lane = lax.broadcasted_iota(jnp.int32, q.shape, 1)
    zero = jnp.zeros_like(q)
    q1 = jnp.where(lane < DA, q, zero)
    q2 = jnp.where(lane < DA, zero, q)
    o1 = _softmax_pv(lax.dot_general(q1, k, _NT, preferred_element_type=F32), v, 1.0)
    o2 = _softmax_pv(lax.dot_general(q2, k, _NT, preferred_element_type=F32), v, 1.0)
    lv = lam_ref[...]
    lam = (jnp.exp(jnp.sum(lv[0:1] * lv[1:2], axis=-1, keepdims=True))
           - jnp.exp(jnp.sum(lv[2:3] * lv[3:4], axis=-1, keepdims=True)) + LAMBDA_INIT)
    o = o1 - lam * o2
    y = o * lax.rsqrt(jnp.mean(o * o, axis=-1, keepdims=True) + EPS) * g_ref[...]
    o_ref[0] = (y * (1.0 - LAMBDA_INIT)).astype(BF16)


def _diff_attention(qa, ka, va, lam_rows, subln_g):
    bsz, n, w = qa.shape
    s = ka.shape[1]
    tq = Q_TILE
    return pl.pallas_call(
        _diff_attn_kernel,
        out_shape=jax.ShapeDtypeStruct((bsz, n, w), BF16),
        grid=(bsz, HA, n // tq),
        in_specs=[
            pl.BlockSpec((1, tq, LANES), lambda b, h, i: (b, i, h)),
            pl.BlockSpec((1, s, LANES), lambda b, h, i: (b, 0, h)),
            pl.BlockSpec((1, s, LANES), lambda b, h, i: (b, 0, h)),
            _full(lam_rows.shape), _full(subln_g.shape),
        ],
        out_specs=pl.BlockSpec((1, tq, LANES), lambda b, h, i: (b, i, h)),
        compiler_params=_cparams(("parallel", "parallel", "arbitrary")),
        name="diff_attention",
    )(qa, ka, va, lam_rows, subln_g)


def _mla_kernel(q_ref, k_ref, v_ref, o_ref):
    s = lax.dot_general(q_ref[0], k_ref[0], _NT, preferred_element_type=F32)
    o_ref[0] = _softmax_pv(s, v_ref[0], (NOPE + ROPE) ** -0.5).astype(BF16)


def _mla_attention(qm, km, vb):
    bsz, n, _ = qm.shape
    s = km.shape[1]
    tq = Q_TILE
    return pl.pallas_call(
        _mla_kernel,
        out_shape=jax.ShapeDtypeStruct((bsz, n, HB * VB), BF16),
        grid=(bsz, HB, n // tq),
        in_specs=[
            pl.BlockSpec((1, tq, 256), lambda b, h, i: (b, i, h)),
            pl.BlockSpec((1, s, 256), lambda b, h, i: (b, 0, h)),
            pl.BlockSpec((1, s, VB), lambda b, h, i: (b, 0, h)),
        ],
        out_specs=pl.BlockSpec((1, tq, VB), lambda b, h, i: (b, i, h)),
        compiler_params=_cparams(("parallel", "parallel", "arbitrary")),
        name="latent_attention",
    )(qm, km, vb)


def _layer_norm(r, g, b):
    mu = jnp.mean(r, axis=-1, keepdims=True)
    c = r - mu
    var = jnp.mean(c * c, axis=-1, keepdims=True)
    return c * lax.rsqrt(var + EPS) * g + b


def _merge_kernel(ya_ref, yb_ref, gs_ref, x_ref, wpa_ref, wpb_ref, wo_ref, gt1_ref, sc2_ref, sh2_ref,
                  g1_ref, b1_ref, wpq_ref, keys_ref, x1_ref, h2_ref, sct_ref):
    d = x_ref.shape[-1]
    gs = gs_ref[...].astype(F32)
    za = jnp.dot(ya_ref[...], wpa_ref[...], preferred_element_type=F32)
    zb = jnp.dot(yb_ref[...], wpb_ref[...], preferred_element_type=F32)
    z = gs[:, :d] * za + gs[:, d:] * zb
    y = jnp.dot(z.astype(BF16), wo_ref[...], preferred_element_type=F32)
    x1 = _layer_norm(DEEPNORM_ALPHA * x_ref[...] + gt1_ref[0] * y, g1_ref[...], b1_ref[...])
    x1_ref[...] = x1
    h2 = x1 * (1.0 + sc2_ref[0]) + sh2_ref[0]
    h2_ref[...] = h2
    qp = jnp.dot(h2.astype(BF16), wpq_ref[...], preferred_element_type=F32).astype(BF16)
    for hh in range(2 * HP):
        sct_ref[hh] = lax.dot_general(keys_ref[hh], qp[:, hh * DK_HALF:(hh + 1) * DK_HALF], _NT,
                                      preferred_element_type=F32)


def _merge(ya, yb, gs, x2d, wpa, wpb, wo, gt1, sc2, sh2, ln1_g, ln1_b, wpq, keys, tiles_per_batch):
    t, d = x2d.shape
    tn = TOKEN_TILE
    row = lambda i: (i, 0)
    brow = lambda i: (i // tiles_per_batch, 0, 0)
    return pl.pallas_call(
        _merge_kernel,
        out_shape=(jax.ShapeDtypeStruct((t, d), F32), jax.ShapeDtypeStruct((t, d), F32),
                   jax.ShapeDtypeStruct((2 * HP, N_KEYS, t), F32)),
        grid=(t // tn,),
        in_specs=[
            pl.BlockSpec((tn, d), row), pl.BlockSpec((tn, d), row), pl.BlockSpec((tn, 2 * d), row),
            pl.BlockSpec((tn, d), row),
            _resident(wpa.shape), _resident(wpb.shape), _resident(wo.shape),
            pl.BlockSpec((1, 1, d), brow), pl.BlockSpec((1, 1, d), brow), pl.BlockSpec((1, 1, d), brow),
            _full((1, d)), _full((1, d)), _resident(wpq.shape), _resident(keys.shape),
        ],
        out_specs=(pl.BlockSpec((tn, d), row), pl.BlockSpec((tn, d), row),
                   pl.BlockSpec((2 * HP, N_KEYS, tn), lambda i: (0, 0, i))),
        compiler_params=_cparams(("parallel",)),
        name="merge",
    )(ya, yb, gs, x2d, wpa, wpb, wo, gt1, sc2, sh2, ln1_g, ln1_b, wpq, keys)


def _top_rows(s, payload, k):
    nrows = s.shape[0]
    row = lax.broadcasted_iota(jnp.int32, s.shape, 0)
    vals, pays = [], []
    for _ in range(k):
        m = jnp.max(s, axis=0, keepdims=True)
        pos = jnp.min(jnp.where(s == m, row, nrows), axis=0, keepdims=True)
        sel = row == pos
        vals.append(m)
        pays.append(jnp.max(jnp.where(sel, payload, -1), axis=0, keepdims=True))
        s = jnp.where(sel, -jnp.inf, s)
    return jnp.concatenate(vals, axis=0), jnp.concatenate(pays, axis=0)


def _route_kernel(sct_ref, idx_ref, g_ref):
    tl = sct_ref.shape[-1]
    key_row = lax.broadcasted_iota(jnp.int32, (N_KEYS, tl), 0)
    idx_rows, g_rows = [], []
    for hh in range(HP):
        v1, i1 = _top_rows(sct_ref[2 * hh], key_row, TOPK_HALF)
        v2, i2 = _top_rows(sct_ref[2 * hh + 1], key_row, TOPK_HALF)
        cand = jnp.concatenate([v1[a:a + 1] + v2 for a in range(TOPK_HALF)], axis=0)
        cidx = jnp.concatenate([i1[a:a + 1] * N_KEYS + i2 for a in range(TOPK_HALF)], axis=0)
        gv, gi = _top_rows(cand, cidx, TOPK)
        e = jnp.exp(gv - gv[0:1])
        g_rows.append(e / jnp.sum(e, axis=0, keepdims=True))
        idx_rows.append(gi)
    g_ref[...] = jnp.concatenate(g_rows, axis=0)
    idx_t = jnp.concatenate(idx_rows, axis=0)
    idx_ref[...] = jnp.transpose(idx_t.astype(F32)).astype(jnp.int32)


def _routing(sct):
    _, _, t = sct.shape
    tl = ROUTE_TILE
    nsel = HP * TOPK
    return pl.pallas_call(
        _route_kernel,
        out_shape=(jax.ShapeDtypeStruct((t, nsel), jnp.int32), jax.ShapeDtypeStruct((nsel, t), F32)),
        grid=(t // tl,),
        in_specs=[pl.BlockSpec((2 * HP, N_KEYS, tl), lambda i: (0, 0, i))],
        out_specs=(pl.BlockSpec((tl, nsel), lambda i: (i, 0)), pl.BlockSpec((nsel, tl), lambda i: (0, i))),
        compiler_params=_cparams(("parallel",)),
        name="routing",
    )(sct)


_HI_MASK = 0xFFFF0000


def _unpack(slab):
    lo = lax.bitcast_convert_type(slab << 16, F32)
    hi = lax.bitcast_convert_type(slab & jnp.uint32(_HI_MASK), F32)
    return lo, hi


def _expert_down_kernel(idx_ref, x_ref, g_ref, u_ref, w_ref, prod_ref):
    tb = x_ref.shape[0]
    nsel = idx_ref.shape[1]
    lane = lax.broadcasted_iota(jnp.int32, (nsel, tb), 1)

    def token(t, at):
        xv = x_ref[t]
        xl, xh = xv[0:4], xv[4:8]
        for j in range(nsel):
            lo, hi = _unpack(u_ref[idx_ref[t, j]])
            prod_ref[4 * j:4 * j + 4, :] = lo * xl + hi * xh
        ps = (prod_ref[pl.ds(0, nsel, stride=4), :] + prod_ref[pl.ds(1, nsel, stride=4), :]
              + prod_ref[pl.ds(2, nsel, stride=4), :] + prod_ref[pl.ds(3, nsel, stride=4), :])
        a = jnp.sum(ps, axis=1, keepdims=True)
        return jnp.where(lane == t, a, at)

    at = lax.fori_loop(0, tb, token, jnp.zeros((nsel, tb), F32))
    w_ref[...] = jnp.transpose(jax.nn.gelu(at) * g_ref[...])


def _expert_down(idx, x3, g_t, u_tab):
    t = x3.shape[0]
    tb = EXPERT_TILE
    nsel = idx.shape[1]
    return pl.pallas_call(
        _expert_down_kernel,
        out_shape=jax.ShapeDtypeStruct((t, nsel), F32),
        grid=(t // tb,),
        in_specs=[
            pl.BlockSpec((tb, nsel), lambda i: (i, 0), memory_space=pltpu.SMEM),
            pl.BlockSpec((tb, 8, LANES), lambda i: (i, 0, 0)),
            pl.BlockSpec((nsel, tb), lambda i: (0, i)),
            _resident(u_tab.shape),
        ],
        out_specs=pl.BlockSpec((tb, nsel), lambda i: (i, 0)),
        scratch_shapes=[pltpu.VMEM((4 * nsel, LANES), F32)],
        compiler_params=_cparams(("arbitrary",)),
        name="expert_down",
    )(idx, x3, g_t, u_tab)


def _expert_up_kernel(idx_ref, w_ref, x1_ref, v_ref, gt2_ref, g2_ref, b2_ref, o_ref, acc_ref):
    tb = x1_ref.shape[0]
    nsel = idx_ref.shape[1]
    nacc = 4

    def token(t, carry):
        los = [jnp.zeros((4, LANES), F32) for _ in range(nacc)]
        his = [jnp.zeros((4, LANES), F32) for _ in range(nacc)]
        for j in range(nsel):
            lo, hi = _unpack(v_ref[idx_ref[t, j]])
            wj = w_ref[t, j]
            los[j % nacc] = los[j % nacc] + wj * lo
            his[j % nacc] = his[j % nacc] + wj * hi
        acc_ref[t, 0:4, :] = (los[0] + los[1]) + (los[2] + los[3])
        acc_ref[t, 4:8, :] = (his[0] + his[1]) + (his[2] + his[3])
        return carry

    lax.fori_loop(0, tb, token, 0)
    r = DEEPNORM_ALPHA * x1_ref[...] + gt2_ref[...] * acc_ref[...]
    n = r.shape[1] * r.shape[2]
    mu = jnp.sum(jnp.sum(r, axis=2, keepdims=True), axis=1, keepdims=True) / n
    c = r - mu
    var = jnp.sum(jnp.sum(c * c, axis=2, keepdims=True), axis=1, keepdims=True) / n
    o_ref[...] = c * lax.rsqrt(var + EPS) * g2_ref[...] + b2_ref[...]


def _expert_up(idx, w, x13, v_tab, gt2, ln2_g, ln2_b, tiles_per_batch):
    t = x13.shape[0]
    tb = EXPERT_TILE
    nsel = idx.shape[1]
    smem = lambda: pl.BlockSpec((tb, nsel), lambda i: (i, 0), memory_space=pltpu.SMEM)
    tok = pl.BlockSpec((tb, 8, LANES), lambda i: (i, 0, 0))
    return pl.pallas_call(
        _expert_up_kernel,
        out_shape=jax.ShapeDtypeStruct(x13.shape, F32),
        grid=(t // tb,),
        in_specs=[smem(), smem(), tok, _resident(v_tab.shape),
                  pl.BlockSpec((1, 8, LANES), lambda i: (i // tiles_per_batch, 0, 0)),
                  _full((1, 8, LANES)), _full((1, 8, LANES))],
        out_specs=tok,
        scratch_shapes=[pltpu.VMEM((tb, 8, LANES), F32)],
        compiler_params=_cparams(("arbitrary",)),
        name="expert_up",
    )(idx, w, x13, v_tab, gt2, ln2_g, ln2_b)


def _pack_table(tab):
    e, d = tab.shape
    bits = lax.bitcast_convert_type(tab.astype(BF16), jnp.uint16).astype(jnp.uint32)
    half = d // 2
    word = bits[:, :half] | (bits[:, half:] << 16)
    return word.reshape(e, half // LANES, LANES)


def _rope_tables(n, nctx):
    rows = n // GRID_W
    quarter = DA // 4
    inv = ROPE_BASE ** (-jnp.arange(quarter, dtype=F32) / quarter)
    row_ang = jnp.arange(rows, dtype=F32)[:, None] * inv
    col_ang = jnp.arange(GRID_W, dtype=F32)[:, None] * inv
    ang = jnp.concatenate([
        jnp.broadcast_to(row_ang[:, None, :], (rows, GRID_W, quarter)),
        jnp.broadcast_to(col_ang[None, :, :], (rows, GRID_W, quarter))], -1).reshape(n, 2 * quarter)
    cos = jnp.concatenate([jnp.ones((nctx, 2 * quarter), F32), jnp.cos(ang)], 0)
    sin = jnp.concatenate([jnp.zeros((nctx, 2 * quarter), F32), jnp.sin(ang)], 0)
    one, zero = jnp.ones_like(cos), jnp.zeros_like(sin)
    cf = jnp.concatenate([cos, cos, cos, cos], 1)
    sf = jnp.concatenate([-sin, sin, -sin, sin], 1)
    ch = jnp.concatenate([cos, cos, one, one], 1)
    sh = jnp.concatenate([-sin, sin, zero, zero], 1)
    return cf, sf, ch, sh


def kernel(x, c, ctx, c_ctx, w_mod, b_mod, w_in, w_q_up, q_norm_g, w_kv_up, kv_norm_g, lambda_q1, lambda_k1,
           lambda_q2, lambda_k2, subln_g, w_pa, w_pb, w_out, ln1_g, ln1_b, w_pq, peer_keys, peer_u, peer_v,
           ln2_g, ln2_b):
    bsz, n, d = x.shape
    nctx = ctx.shape[1]
    t = bsz * n
    assert d == 8 * LANES and n % TOKEN_TILE == 0 and nctx % TOKEN_TILE == 0 and n % GRID_W == 0
    assert t % ROUTE_TILE == 0 and n % EXPERT_TILE == 0 and w_mod.shape[0] == DEPTH

    rows = -(-(bsz + 1) // 8) * 8
    c_rows = jnp.concatenate([c, c_ctx[None, :], jnp.zeros((rows - bsz - 1, d), F32)], 0)
    mod = _modulation(c_rows, w_mod[0], b_mod[0][None, :])[:bsz + 1]
    sh1, sc1, gt1, sh2, sc2, gt2 = [m[:, None, :] for m in jnp.split(mod, 6, axis=-1)]

    w_all = jnp.concatenate([w_in[0], jnp.zeros((d, LANES - ROPE), F32)], 1).astype(BF16)
    wq = w_q_up[0].reshape(Q_LORA, HB, NOPE + ROPE)
    wq = jnp.concatenate([wq, jnp.zeros((Q_LORA, HB, 256 - NOPE - ROPE), F32)], -1).reshape(Q_LORA, HB * 256)
    cf, sf, ch, shh = _rope_tables(n, nctx)
    lam_rows = jnp.zeros((8, LANES), F32).at[0:4, 0:DA].set(
        jnp.stack([lambda_q1[0], lambda_k1[0], lambda_q2[0], lambda_k2[0]]))

    qa, qm, gs, ka, va, km, vb = _projection(
        x, ctx, sc1, sh1, w_all, q_norm_g[0][None, :], kv_norm_g[0][None, :], wq.astype(BF16),
        w_kv_up[0].astype(BF16), cf, sf, ch, shh)

    ya = _diff_attention(qa, ka, va, lam_rows, subln_g[0][None, :])
    yb = _mla_attention(qm, km, vb)

    keys = peer_keys[0].reshape(2 * HP, N_KEYS, DK_HALF).astype(BF16)
    x1, h2, sct = _merge(
        ya.reshape(t, d), yb.reshape(t, d), gs.reshape(t, 2 * d), x.reshape(t, d),
        w_pa[0].astype(BF16), w_pb[0].astype(BF16), w_out[0].astype(BF16),
        gt1[:bsz], sc2[:bsz], sh2[:bsz], ln1_g[0][None, :], ln1_b[0][None, :],
        w_pq[0].astype(BF16), keys, n // TOKEN_TILE)

    idx, g_t = _routing(sct)

    w = _expert_down(idx, h2.reshape(t, 8, LANES), g_t, _pack_table(peer_u[0]))
    out = _expert_up(idx, w, x1.reshape(t, 8, LANES), _pack_table(peer_v[0]),
                     gt2[:bsz].reshape(bsz, 8, LANES), ln2_g[0].reshape(1, 8, LANES),
                     ln2_b[0].reshape(1, 8, LANES), n // EXPERT_TILE)
    return out.reshape(bsz, n, d)
```

```python
import functools
import math

import jax
import jax.numpy as jnp
from jax import lax
from jax.experimental import pallas as pl
from jax.experimental.pallas import tpu as pltpu

GRID_W = 64
HA, DA = 8, 64
HB, NOPE, ROPE, VB = 8, 128, 64, 128
Q_LORA, KV_LORA = 256, 256
N_KEYS = 128
HP, DK_HALF = 8, 128
TOPK_HALF, TOPK = 16, 16
ROPE_BASE = 10000.0
EPS = 1e-6
DEPTH = 1
DEEPNORM_ALPHA = (2 * DEPTH) ** 0.25
LAMBDA_INIT = 0.8 - 0.6 * math.exp(-0.3 * 0)

LANES = 128
VMEM_LIMIT_BYTES = 56 * 1024 * 1024

TOKEN_TILE = 256
Q_TILE = 256
ROUTE_TILE = 256
EXPERT_TILE = 128

F32 = jnp.float32
BF16 = jnp.bfloat16


def _cparams(sem):
    return pltpu.CompilerParams(dimension_semantics=sem, vmem_limit_bytes=VMEM_LIMIT_BYTES)


def _full(shape):
    n = len(shape)
    return pl.BlockSpec(shape, lambda *_: (0,) * n)


def _resident(shape):
    n = len(shape)
    return pl.BlockSpec(shape, lambda *_: (0,) * n, pipeline_mode=pl.Buffered(1))


def _mod_kernel(c_ref, w_ref, b_ref, o_ref):
    c = c_ref[...]
    s = c * jax.nn.sigmoid(c)
    o_ref[...] = jnp.dot(s, w_ref[...], preferred_element_type=F32) + b_ref[...]


def _modulation(c_rows, w_mod, b_mod):
    r, d = c_rows.shape
    n = w_mod.shape[1]
    tn = 1536
    return pl.pallas_call(
        _mod_kernel,
        out_shape=jax.ShapeDtypeStruct((r, n), F32),
        grid=(n // tn,),
        in_specs=[_full((r, d)), pl.BlockSpec((d, tn), lambda j: (0, j)), pl.BlockSpec((1, tn), lambda j: (0, j))],
        out_specs=pl.BlockSpec((r, tn), lambda j: (0, j)),
        compiler_params=_cparams(("arbitrary",)),
        name="modulation",
    )(c_rows, w_mod, b_mod)


def _swap_halves(blk):
    lane = lax.broadcasted_iota(jnp.int32, blk.shape, 1)
    up = pltpu.roll(blk, LANES - 32, axis=1)
    dn = pltpu.roll(blk, 32, axis=1)
    return jnp.where((lane % 64) < 32, up, dn)


def _rope_block(blk, c, s):
    return blk * c + _swap_halves(blk) * s


def _rms(x, g):
    return x * lax.rsqrt(jnp.mean(x * x, axis=-1, keepdims=True) + EPS) * g


def _proj_kernel(nct, x_ref, ctx_ref, sc_ref, sh_ref, w_ref, qg_ref, kvg_ref, wq_ref, wkv_ref,
                 cf_ref, sf_ref, ch_ref, shh_ref,
                 qa_ref, qm_ref, gs_ref, ka_ref, va_ref, km_ref, vb_ref):
    i = pl.program_id(1)
    d = x_ref.shape[-1]
    xt = jnp.where(i < nct, ctx_ref[0], x_ref[0])
    h = xt * (1.0 + sc_ref[0]) + sh_ref[0]
    hb = h.astype(BF16)
    cf, sf = cf_ref[...], sf_ref[...]
    ch, shh = ch_ref[...], shh_ref[...]

    def proj(lo, hi):
        return jnp.dot(hb, w_ref[:, lo:hi], preferred_element_type=F32)

    c_qa, c_ql, c_g = 0, HA * 2 * DA, HA * 2 * DA + Q_LORA
    c_ka = c_g + 2 * d
    c_va = c_ka + HA * 2 * DA
    c_kvl = c_va + HA * 2 * DA
    c_kr = c_kvl + KV_LORA

    ka = proj(c_ka, c_va)
    for b in range(ka.shape[1] // LANES):
        sl = slice(b * LANES, (b + 1) * LANES)
        ka_ref[0, :, sl] = _rope_block(ka[:, sl], cf, sf).astype(BF16)
    va_ref[0] = proj(c_va, c_kvl).astype(BF16)
    kvl = _rms(proj(c_kvl, c_kr), kvg_ref[...]).astype(BF16)
    kv = jnp.dot(kvl, wkv_ref[...], preferred_element_type=F32)
    kr = _rope_block(proj(c_kr, c_kr + LANES), ch, shh).astype(BF16)
    for hh in range(HB):
        base = hh * (NOPE + VB)
        km_ref[0, :, hh * 256:hh * 256 + NOPE] = kv[:, base:base + NOPE].astype(BF16)
        km_ref[0, :, hh * 256 + NOPE:(hh + 1) * 256] = kr
        vb_ref[0, :, hh * VB:(hh + 1) * VB] = kv[:, base + NOPE:base + NOPE + VB].astype(BF16)

    @pl.when(i >= nct)
    def _():
        qa = proj(c_qa, c_ql)
        for b in range(qa.shape[1] // LANES):
            sl = slice(b * LANES, (b + 1) * LANES)
            qa_ref[0, :, sl] = (_rope_block(qa[:, sl], cf, sf) * (DA ** -0.5)).astype(BF16)
        ql = _rms(proj(c_ql, c_g), qg_ref[...]).astype(BF16)
        qb = jnp.dot(ql, wq_ref[...], preferred_element_type=F32)
        for hh in range(HB):
            qm_ref[0, :, hh * 256:hh * 256 + NOPE] = qb[:, hh * 256:hh * 256 + NOPE].astype(BF16)
            qm_ref[0, :, hh * 256 + NOPE:(hh + 1) * 256] = _rope_block(
                qb[:, hh * 256 + NOPE:(hh + 1) * 256], ch, shh).astype(BF16)
        gs_ref[0] = jax.nn.sigmoid(proj(c_g, c_ka)).astype(BF16)


def _projection(x, ctx, sc1, sh1, w_all, q_norm_g, kv_norm_g, wq, wkv, cf, sf, ch, shh):
    bsz, n, d = x.shape
    nctx = ctx.shape[1]
    tn = TOKEN_TILE
    nct = nctx // tn
    s = nctx + n
    ncols = w_all.shape[1]

    def lat(b, i):
        return (b, jnp.maximum(i - nct, 0), 0)

    def cx(b, i):
        return (b, jnp.minimum(i, nct - 1), 0)

    def modrow(b, i):
        return (jnp.where(i < nct, bsz, b), 0, 0)

    def allk(b, i):
        return (b, i, 0)

    def tab(b, i):
        return (i, 0)

    bf = lambda w: jax.ShapeDtypeStruct((bsz, n, w), BF16)
    bfs = lambda w: jax.ShapeDtypeStruct((bsz, s, w), BF16)
    return pl.pallas_call(
        functools.partial(_proj_kernel, nct),
        out_shape=(bf(HA * 2 * DA), bf(HB * 256), bf(2 * d), bfs(HA * 2 * DA), bfs(HA * 2 * DA), bfs(HB * 256), bfs(HB * VB)),
        grid=(bsz, s // tn),
        in_specs=[
            pl.BlockSpec((1, tn, d), lat), pl.BlockSpec((1, tn, d), cx),
            pl.BlockSpec((1, 1, d), modrow), pl.BlockSpec((1, 1, d), modrow),
            _resident((d, ncols)), _full((1, Q_LORA)), _full((1, KV_LORA)),
            _resident(wq.shape), _resident(wkv.shape),
            pl.BlockSpec((tn, LANES), tab), pl.BlockSpec((tn, LANES), tab),
            pl.BlockSpec((tn, LANES), tab), pl.BlockSpec((tn, LANES), tab),
        ],
        out_specs=(
            pl.BlockSpec((1, tn, HA * 2 * DA), lat), pl.BlockSpec((1, tn, HB * 256), lat),
            pl.BlockSpec((1, tn, 2 * d), lat),
            pl.BlockSpec((1, tn, HA * 2 * DA), allk), pl.BlockSpec((1, tn, HA * 2 * DA), allk),
            pl.BlockSpec((1, tn, HB * 256), allk), pl.BlockSpec((1, tn, HB * VB), allk),
        ),
        compiler_params=_cparams(("parallel", "arbitrary")),
        name="projection",
    )(x, ctx, sc1, sh1, w_all, q_norm_g, kv_norm_g, wq, wkv, cf, sf, ch, shh)


_NT = (((1,), (1,)), ((), ()))


def _softmax_pv(s, v, scale):
    m = jnp.max(s, axis=-1, keepdims=True)
    p = jnp.exp((s - m) * scale)
    l = jnp.sum(p, axis=-1, keepdims=True)
    o = jnp.dot(p.astype(BF16), v, preferred_element_type=F32)
    return o / l


def _diff_attn_kernel(q_ref, k_ref, v_ref, lam_ref, g_ref, o_ref):
    q = q_ref[0]
    k = k_ref[0]
    v = v_ref[0]
    lane = lax.broadcasted_iota(jnp.int32, q.shape, 1)
    zero = jnp.zeros_like(q)
    q1 = jnp.where(lane < DA, q, zero)
    q2 = jnp.where(lane < DA, zero, q)
    o1 = _softmax_pv(lax.dot_general(q1, k, _NT, preferred_element_type=F32), v, 1.0)
    o2 = _softmax_pv(lax.dot_general(q2, k, _NT, preferred_element_type=F32), v, 1.0)
    lv = lam_ref[...]
    lam = (jnp.exp(jnp.sum(lv[0:1] * lv[1:2], axis=-1, keepdims=True))
           - jnp.exp(jnp.sum(lv[2:3] * lv[3:4], axis=-1, keepdims=True)) + LAMBDA_INIT)
    o = o1 - lam * o2
    y = o * lax.rsqrt(jnp.mean(o * o, axis=-1, keepdims=True) + EPS) * g_ref[...]
    o_ref[0] = (y * (1.0 - LAMBDA_INIT)).astype(BF16)


def _diff_attention(qa, ka, va, lam_rows, subln_g):
    bsz, n, w = qa.shape
    s = ka.shape[1]
    tq = Q_TILE
    return pl.pallas_call(
        _diff_attn_kernel,
        out_shape=jax.ShapeDtypeStruct((bsz, n, w), BF16),
        grid=(bsz, HA, n // tq),
        in_specs=[
            pl.BlockSpec((1, tq, LANES), lambda b, h, i: (b, i, h)),
            pl.BlockSpec((1, s, LANES), lambda b, h, i: (b, 0, h)),
            pl.BlockSpec((1, s, LANES), lambda b, h, i: (b, 0, h)),
            _full(lam_rows.shape), _full(subln_g.shape),
        ],
        out_specs=pl.BlockSpec((1, tq, LANES), lambda b, h, i: (b, i, h)),
        compiler_params=_cparams(("parallel", "parallel", "arbitrary")),
        name="diff_attention",
    )(qa, ka, va, lam_rows, subln_g)


def _mla_kernel(q_ref, k_ref, v_ref, o_ref):
    s = lax.dot_general(q_ref[0], k_ref[0], _NT, preferred_element_type=F32)
    o_ref[0] = _softmax_pv(s, v_ref[0], (NOPE + ROPE) ** -0.5).astype(BF16)


def _mla_attention(qm, km, vb):
    bsz, n, _ = qm.shape
    s = km.shape[1]
    tq = Q_TILE
    return pl.pallas_call(
        _mla_kernel,
        out_shape=jax.ShapeDtypeStruct((bsz, n, HB * VB), BF16),
        grid=(bsz, HB, n // tq),
        in_specs=[
            pl.BlockSpec((1, tq, 256), lambda b, h, i: (b, i, h)),
            pl.BlockSpec((1, s, 256), lambda b, h, i: (b, 0, h)),
            pl.BlockSpec((1, s, VB), lambda b, h, i: (b, 0, h)),
        ],
        out_specs=pl.BlockSpec((1, tq, VB), lambda b, h, i: (b, i, h)),
        compiler_params=_cparams(("parallel", "parallel", "arbitrary")),
        name="latent_attention",
    )(qm, km, vb)


def _layer_norm(r, g, b):
    mu = jnp.mean(r, axis=-1, keepdims=True)
    c = r - mu
    var = jnp.mean(c * c, axis=-1, keepdims=True)
    return c * lax.rsqrt(var + EPS) * g + b


def _merge_kernel(ya_ref, yb_ref, gs_ref, x_ref, wpa_ref, wpb_ref, wo_ref, gt1_ref, sc2_ref, sh2_ref,
                  g1_ref, b1_ref, wpq_ref, keys_ref, x1_ref, h2_ref, sct_ref):
    d = x_ref.shape[-1]
    gs = gs_ref[...].astype(F32)
    za = jnp.dot(ya_ref[...], wpa_ref[...], preferred_element_type=F32)
    zb = jnp.dot(yb_ref[...], wpb_ref[...], preferred_element_type=F32)
    z = gs[:, :d] * za + gs[:, d:] * zb
    y = jnp.dot(z.astype(BF16), wo_ref[...], preferred_element_type=F32)
    x1 = _layer_norm(DEEPNORM_ALPHA * x_ref[...] + gt1_ref[0] * y, g1_ref[...], b1_ref[...])
    x1_ref[...] = x1
    h2 = x1 * (1.0 + sc2_ref[0]) + sh2_ref[0]
    h2_ref[...] = h2
    qp = jnp.dot(h2.astype(BF16), wpq_ref[...], preferred_element_type=F32).astype(BF16)
    for hh in range(2 * HP):
        sct_ref[hh] = lax.dot_general(keys_ref[hh], qp[:, hh * DK_HALF:(hh + 1) * DK_HALF], _NT,
                                      preferred_element_type=F32)


def _merge(ya, yb, gs, x2d, wpa, wpb, wo, gt1, sc2, sh2, ln1_g, ln1_b, wpq, keys, tiles_per_batch):
    t, d = x2d.shape
    tn = TOKEN_TILE
    row = lambda i: (i, 0)
    brow = lambda i: (i // tiles_per_batch, 0, 0)
    return pl.pallas_call(
        _merge_kernel,
        out_shape=(jax.ShapeDtypeStruct((t, d), F32), jax.ShapeDtypeStruct((t, d), F32),
                   jax.ShapeDtypeStruct((2 * HP, N_KEYS, t), F32)),
        grid=(t // tn,),
        in_specs=[
            pl.BlockSpec((tn, d), row), pl.BlockSpec((tn, d), row), pl.BlockSpec((tn, 2 * d), row),
            pl.BlockSpec((tn, d), row),
            _resident(wpa.shape), _resident(wpb.shape), _resident(wo.shape),
            pl.BlockSpec((1, 1, d), brow), pl.BlockSpec((1, 1, d), brow), pl.BlockSpec((1, 1, d), brow),
            _full((1, d)), _full((1, d)), _resident(wpq.shape), _resident(keys.shape),
        ],
        out_specs=(pl.BlockSpec((tn, d), row), pl.BlockSpec((tn, d), row),
                   pl.BlockSpec((2 * HP, N_KEYS, tn), lambda i: (0, 0, i))),
        compiler_params=_cparams(("parallel",)),
        name="merge",
    )(ya, yb, gs, x2d, wpa, wpb, wo, gt1, sc2, sh2, ln1_g, ln1_b, wpq, keys)


def _top_rows(s, payload, k):
    nrows = s.shape[0]
    row = lax.broadcasted_iota(jnp.int32, s.shape, 0)
    vals, pays = [], []
    for _ in range(k):
        m = jnp.max(s, axis=0, keepdims=True)
        pos = jnp.min(jnp.where(s == m, row, nrows), axis=0, keepdims=True)
        sel = row == pos
        vals.append(m)
        pays.append(jnp.max(jnp.where(sel, payload, -1), axis=0, keepdims=True))
        s = jnp.where(sel, -jnp.inf, s)
    return jnp.concatenate(vals, axis=0), jnp.concatenate(pays, axis=0)


def _route_kernel(sct_ref, idx_ref, g_ref):
    tl = sct_ref.shape[-1]
    key_row = lax.broadcasted_iota(jnp.int32, (N_KEYS, tl), 0)
    idx_rows, g_rows = [], []
    for hh in range(HP):
        v1, i1 = _top_rows(sct_ref[2 * hh], key_row, TOPK_HALF)
        v2, i2 = _top_rows(sct_ref[2 * hh + 1], key_row, TOPK_HALF)
        cand = jnp.concatenate([v1[a:a + 1] + v2 for a in range(TOPK_HALF)], axis=0)
        cidx = jnp.concatenate([(i1[a:a + 1] * N_KEYS + i2) * SLAB_ROWS for a in range(TOPK_HALF)], axis=0)
        gv, gi = _top_rows(cand, cidx, TOPK)
        e = jnp.exp(gv - gv[0:1])
        g_rows.append(e / jnp.sum(e, axis=0, keepdims=True))
        idx_rows.append(gi)
    g_ref[...] = jnp.concatenate(g_rows, axis=0)
    idx_t = jnp.concatenate(idx_rows, axis=0)
    idx_ref[...] = jnp.transpose(idx_t.astype(F32)).astype(jnp.int32)


def _routing(sct):
    _, _, t = sct.shape
    tl = ROUTE_TILE
    nsel = HP * TOPK
    return pl.pallas_call(
        _route_kernel,
        out_shape=(jax.ShapeDtypeStruct((t, nsel), jnp.int32), jax.ShapeDtypeStruct((nsel, t), F32)),
        grid=(t // tl,),
        in_specs=[pl.BlockSpec((2 * HP, N_KEYS, tl), lambda i: (0, 0, i))],
        out_specs=(pl.BlockSpec((tl, nsel), lambda i: (i, 0)), pl.BlockSpec((nsel, tl), lambda i: (0, i))),
        compiler_params=_cparams(("parallel",)),
        name="routing",
    )(sct)


SLAB_ROWS = 4
_HI_MASK = 0xFFFF0000


def _unpack(slab):
    lo = lax.bitcast_convert_type(slab << 16, F32)
    hi = lax.bitcast_convert_type(slab & jnp.uint32(_HI_MASK), F32)
    return lo, hi


def _gather_slabs(idx_ref, tab_ref, slab_ref, t):
    group = 16
    for j0 in range(0, idx_ref.shape[1], group):
        row = idx_ref.at[t, pl.ds(j0, group)]
        for k in range(group):
            j = j0 + k
            first = pl.multiple_of(row[k], SLAB_ROWS)
            slab_ref[SLAB_ROWS * j:SLAB_ROWS * (j + 1), :] = tab_ref[pl.ds(first, SLAB_ROWS), :]


def _two_phase_loop(tb, gather, math, init):
    gather(0, 0)
    gather(1, 1)

    def two_tokens(i, carry):
        t0 = 2 * i
        carry = math(0, t0, carry)
        carry = math(1, t0 + 1, carry)
        gather(0, jnp.minimum(t0 + 2, tb - 1))
        gather(1, jnp.minimum(t0 + 3, tb - 1))
        return carry

    return lax.fori_loop(0, tb // 2, two_tokens, init)


def _slab_rows(slab_ref, s, nsel):
    return slab_ref[pl.ds(s, nsel, stride=4), :]


def _expert_down_kernel(idx_ref, x_ref, g_ref, u_ref, w_ref, slab0_ref, slab1_ref):
    tb = x_ref.shape[0]
    nsel = idx_ref.shape[1]
    slabs = (slab0_ref, slab1_ref)
    lane = lax.broadcasted_iota(jnp.int32, (nsel, tb), 1)

    def gather(buf, t):
        _gather_slabs(idx_ref, u_ref, slabs[buf], t)

    def math(buf, t, at):
        xv = x_ref[t]
        acc = None
        for s in range(4):
            lo, hi = _unpack(_slab_rows(slabs[buf], s, nsel))
            term = lo * xv[s:s + 1] + hi * xv[4 + s:5 + s]
            acc = term if acc is None else acc + term
        a = jnp.sum(acc, axis=1, keepdims=True)
        return jnp.where(lane == t, a, at)

    at = _two_phase_loop(tb, gather, math, jnp.zeros((nsel, tb), F32))
    w_ref[...] = jax.nn.gelu(at) * g_ref[...]


def _expert_down(idx, x3, g_t, u_tab):
    t = x3.shape[0]
    tb = EXPERT_TILE
    nsel = idx.shape[1]
    return pl.pallas_call(
        _expert_down_kernel,
        out_shape=jax.ShapeDtypeStruct((nsel, t), F32),
        grid=(t // tb,),
        in_specs=[
            pl.BlockSpec((tb, nsel), lambda i: (i, 0), memory_space=pltpu.SMEM),
            pl.BlockSpec((tb, 8, LANES), lambda i: (i, 0, 0)),
            pl.BlockSpec((nsel, tb), lambda i: (0, i)),
            _resident(u_tab.shape),
        ],
        out_specs=pl.BlockSpec((nsel, tb), lambda i: (0, i)),
        scratch_shapes=[pltpu.VMEM((4 * nsel, LANES), jnp.uint32), pltpu.VMEM((4 * nsel, LANES), jnp.uint32)],
        compiler_params=_cparams(("arbitrary",)),
        name="expert_down",
    )(idx, x3, g_t, u_tab)


def _expert_up_kernel(idx_ref, wt_ref, x1_ref, v_ref, gt2_ref, g2_ref, b2_ref, o_ref, acc_ref, slab0_ref, slab1_ref):
    tb = x1_ref.shape[0]
    nsel = idx_ref.shape[1]
    slabs = (slab0_ref, slab1_ref)

    def gather(buf, t):
        _gather_slabs(idx_ref, v_ref, slabs[buf], t)

    def math(buf, t, carry):
        wt = wt_ref[...]
        wb = jnp.take_along_axis(wt, jnp.full(wt.shape, t, jnp.int32), axis=1, mode="promise_in_bounds")
        los, his = [], []
        for s in range(4):
            lo, hi = _unpack(_slab_rows(slabs[buf], s, nsel))
            los.append(jnp.sum(lo * wb, axis=0, keepdims=True))
            his.append(jnp.sum(hi * wb, axis=0, keepdims=True))
        acc_ref[t] = jnp.concatenate(los + his, axis=0)
        return carry

    _two_phase_loop(tb, gather, math, 0)
    r = DEEPNORM_ALPHA * x1_ref[...] + gt2_ref[...] * acc_ref[...]
    n = r.shape[1] * r.shape[2]
    mu = jnp.sum(jnp.sum(r, axis=2, keepdims=True), axis=1, keepdims=True) / n
    c = r - mu
    var = jnp.sum(jnp.sum(c * c, axis=2, keepdims=True), axis=1, keepdims=True) / n
    o_ref[...] = c * lax.rsqrt(var + EPS) * g2_ref[...] + b2_ref[...]


def _expert_up(idx, w_t, x13, v_tab, gt2, ln2_g, ln2_b, tiles_per_batch):
    t = x13.shape[0]
    tb = EXPERT_TILE
    nsel = idx.shape[1]
    tok = pl.BlockSpec((tb, 8, LANES), lambda i: (i, 0, 0))
    return pl.pallas_call(
        _expert_up_kernel,
        out_shape=jax.ShapeDtypeStruct(x13.shape, F32),
        grid=(t // tb,),
        in_specs=[pl.BlockSpec((tb, nsel), lambda i: (i, 0), memory_space=pltpu.SMEM),
                  pl.BlockSpec((nsel, tb), lambda i: (0, i)), tok, _resident(v_tab.shape),
                  pl.BlockSpec((1, 8, LANES), lambda i: (i // tiles_per_batch, 0, 0)),
                  _full((1, 8, LANES)), _full((1, 8, LANES))],
        out_specs=tok,
        scratch_shapes=[pltpu.VMEM((tb, 8, LANES), F32),
                        pltpu.VMEM((4 * nsel, LANES), jnp.uint32), pltpu.VMEM((4 * nsel, LANES), jnp.uint32)],
        compiler_params=_cparams(("arbitrary",)),
        name="expert_up",
    )(idx, w_t, x13, v_tab, gt2, ln2_g, ln2_b)


def _pack_table(tab):
    e, d = tab.shape
    bits = lax.bitcast_convert_type(tab.astype(BF16), jnp.uint16).astype(jnp.uint32)
    half = d // 2
    word = bits[:, :half] | (bits[:, half:] << 16)
    assert half == SLAB_ROWS * LANES
    return word.reshape(e * SLAB_ROWS, LANES)


def _rope_tables(n, nctx):
    rows = n // GRID_W
    quarter = DA // 4
    inv = ROPE_BASE ** (-jnp.arange(quarter, dtype=F32) / quarter)
    row_ang = jnp.arange(rows, dtype=F32)[:, None] * inv
    col_ang = jnp.arange(GRID_W, dtype=F32)[:, None] * inv
    ang = jnp.concatenate([
        jnp.broadcast_to(row_ang[:, None, :], (rows, GRID_W, quarter)),
        jnp.broadcast_to(col_ang[None, :, :], (rows, GRID_W, quarter))], -1).reshape(n, 2 * quarter)
    cos = jnp.concatenate([jnp.ones((nctx, 2 * quarter), F32), jnp.cos(ang)], 0)
    sin = jnp.concatenate([jnp.zeros((nctx, 2 * quarter), F32), jnp.sin(ang)], 0)
    one, zero = jnp.ones_like(cos), jnp.zeros_like(sin)
    cf = jnp.concatenate([cos, cos, cos, cos], 1)
    sf = jnp.concatenate([-sin, sin, -sin, sin], 1)
    ch = jnp.concatenate([cos, cos, one, one], 1)
    sh = jnp.concatenate([-sin, sin, zero, zero], 1)
    return cf, sf, ch, sh


def kernel(x, c, ctx, c_ctx, w_mod, b_mod, w_in, w_q_up, q_norm_g, w_kv_up, kv_norm_g, lambda_q1, lambda_k1,
           lambda_q2, lambda_k2, subln_g, w_pa, w_pb, w_out, ln1_g, ln1_b, w_pq, peer_keys, peer_u, peer_v,
           ln2_g, ln2_b):
    bsz, n, d = x.shape
    nctx = ctx.shape[1]
    t = bsz * n
    assert d == 8 * LANES and n % TOKEN_TILE == 0 and nctx % TOKEN_TILE == 0 and n % GRID_W == 0
    assert t % ROUTE_TILE == 0 and n % EXPERT_TILE == 0 and w_mod.shape[0] == DEPTH

    rows = -(-(bsz + 1) // 8) * 8
    c_rows = jnp.concatenate([c, c_ctx[None, :], jnp.zeros((rows - bsz - 1, d), F32)], 0)
    mod = _modulation(c_rows, w_mod[0], b_mod[0][None, :])[:bsz + 1]
    sh1, sc1, gt1, sh2, sc2, gt2 = [m[:, None, :] for m in jnp.split(mod, 6, axis=-1)]

    w_all = jnp.concatenate([w_in[0], jnp.zeros((d, LANES - ROPE), F32)], 1).astype(BF16)
    wq = w_q_up[0].reshape(Q_LORA, HB, NOPE + ROPE)
    wq = jnp.concatenate([wq, jnp.zeros((Q_LORA, HB, 256 - NOPE - ROPE), F32)], -1).reshape(Q_LORA, HB * 256)
    cf, sf, ch, shh = _rope_tables(n, nctx)
    lam_rows = jnp.zeros((8, LANES), F32).at[0:4, 0:DA].set(
        jnp.stack([lambda_q1[0], lambda_k1[0], lambda_q2[0], lambda_k2[0]]))

    qa, qm, gs, ka, va, km, vb = _projection(
        x, ctx, sc1, sh1, w_all, q_norm_g[0][None, :], kv_norm_g[0][None, :], wq.astype(BF16),
        w_kv_up[0].astype(BF16), cf, sf, ch, shh)

    ya = _diff_attention(qa, ka, va, lam_rows, subln_g[0][None, :])
    yb = _mla_attention(qm, km, vb)

    keys = peer_keys[0].reshape(2 * HP, N_KEYS, DK_HALF).astype(BF16)
    x1, h2, sct = _merge(
        ya.reshape(t, d), yb.reshape(t, d), gs.reshape(t, 2 * d), x.reshape(t, d),
        w_pa[0].astype(BF16), w_pb[0].astype(BF16), w_out[0].astype(BF16),
        gt1[:bsz], sc2[:bsz], sh2[:bsz], ln1_g[0][None, :], ln1_b[0][None, :],
        w_pq[0].astype(BF16), keys, n // TOKEN_TILE)

    idx, g_t = _routing(sct)

    w_t = _expert_down(idx, h2.reshape(t, 8, LANES), g_t, _pack_table(peer_u[0]))
    out = _expert_up(idx, w_t, x1.reshape(t, 8, LANES), _pack_table(peer_v[0]),
                     gt2[:bsz].reshape(bsz, 8, LANES), ln2_g[0].reshape(1, 8, LANES),
                     ln2_b[0].reshape(1, 8, LANES), n // EXPERT_TILE)
    return out.reshape(bsz, n, d)
```

```python
import functools
import math

import jax
import jax.numpy as jnp
from jax import lax
from jax.experimental import pallas as pl
from jax.experimental.pallas import tpu as pltpu

GRID_W = 64
HA, DA = 8, 64
HB, NOPE, ROPE, VB = 8, 128, 64, 128
Q_LORA, KV_LORA = 256, 256
N_KEYS = 128
HP, DK_HALF = 8, 128
TOPK_HALF, TOPK = 16, 16
ROPE_BASE = 10000.0
EPS = 1e-6
DEPTH = 1
DEEPNORM_ALPHA = (2 * DEPTH) ** 0.25
LAMBDA_INIT = 0.8 - 0.6 * math.exp(-0.3 * 0)

LANES = 128
VMEM_LIMIT_BYTES = 56 * 1024 * 1024

TOKEN_TILE = 256
Q_TILE = 256
ROUTE_TILE = 256
EXPERT_TILE = 128

F32 = jnp.float32
BF16 = jnp.bfloat16


def _cparams(sem):
    return pltpu.CompilerParams(dimension_semantics=sem, vmem_limit_bytes=VMEM_LIMIT_BYTES)


def _full(shape):
    n = len(shape)
    return pl.BlockSpec(shape, lambda *_: (0,) * n)


def _resident(shape):
    n = len(shape)
    return pl.BlockSpec(shape, lambda *_: (0,) * n, pipeline_mode=pl.Buffered(1))


def _mod_kernel(c_ref, w_ref, b_ref, o_ref):
    c = c_ref[...]
    s = c * jax.nn.sigmoid(c)
    o_ref[...] = jnp.dot(s, w_ref[...], preferred_element_type=F32) + b_ref[...]


def _modulation(c_rows, w_mod, b_mod):
    r, d = c_rows.shape
    n = w_mod.shape[1]
    tn = 1536
    return pl.pallas_call(
        _mod_kernel,
        out_shape=jax.ShapeDtypeStruct((r, n), F32),
        grid=(n // tn,),
        in_specs=[_full((r, d)), pl.BlockSpec((d, tn), lambda j: (0, j)), pl.BlockSpec((1, tn), lambda j: (0, j))],
        out_specs=pl.BlockSpec((r, tn), lambda j: (0, j)),
        compiler_params=_cparams(("arbitrary",)),
        name="modulation",
    )(c_rows, w_mod, b_mod)


def _swap_halves(blk):
    lane = lax.broadcasted_iota(jnp.int32, blk.shape, 1)
    up = pltpu.roll(blk, LANES - 32, axis=1)
    dn = pltpu.roll(blk, 32, axis=1)
    return jnp.where((lane % 64) < 32, up, dn)


def _rope_block(blk, c, s):
    return blk * c + _swap_halves(blk) * s


def _rms(x, g):
    return x * lax.rsqrt(jnp.mean(x * x, axis=-1, keepdims=True) + EPS) * g


def _proj_kernel(nct, x_ref, ctx_ref, sc_ref, sh_ref, w_ref, qg_ref, kvg_ref, wq_ref, wkv_ref,
                 cf_ref, sf_ref, ch_ref, shh_ref,
                 qa_ref, qm_ref, gs_ref, ka_ref, va_ref, km_ref, vb_ref):
    i = pl.program_id(1)
    d = x_ref.shape[-1]
    xt = jnp.where(i < nct, ctx_ref[0], x_ref[0])
    h = xt * (1.0 + sc_ref[0]) + sh_ref[0]
    hb = h.astype(BF16)
    cf, sf = cf_ref[...], sf_ref[...]
    ch, shh = ch_ref[...], shh_ref[...]

    def proj(lo, hi):
        return jnp.dot(hb, w_ref[:, lo:hi], preferred_element_type=F32)

    c_qa, c_ql, c_g = 0, HA * 2 * DA, HA * 2 * DA + Q_LORA
    c_ka = c_g + 2 * d
    c_va = c_ka + HA * 2 * DA
    c_kvl = c_va + HA * 2 * DA
    c_kr = c_kvl + KV_LORA

    ka = proj(c_ka, c_va)
    for b in range(ka.shape[1] // LANES):
        sl = slice(b * LANES, (b + 1) * LANES)
        ka_ref[0, :, sl] = _rope_block(ka[:, sl], cf, sf).astype(BF16)
    lane = lax.broadcasted_iota(jnp.int32, (hb.shape[0], LANES), 1)
    ones_col = jnp.where(lane == 0, 1.0, 0.0).astype(BF16)
    va = proj(c_va, c_kvl).astype(BF16)
    for hh in range(HA):
        va_ref[0, :, hh * 256:hh * 256 + 2 * DA] = va[:, hh * 2 * DA:(hh + 1) * 2 * DA]
        va_ref[0, :, hh * 256 + 2 * DA:(hh + 1) * 256] = ones_col
    kvl = _rms(proj(c_kvl, c_kr), kvg_ref[...]).astype(BF16)
    kv = jnp.dot(kvl, wkv_ref[...], preferred_element_type=F32)
    kr = _rope_block(proj(c_kr, c_kr + LANES), ch, shh).astype(BF16)
    for hh in range(HB):
        base = hh * (NOPE + VB)
        km_ref[0, :, hh * 256:hh * 256 + NOPE] = kv[:, base:base + NOPE].astype(BF16)
        km_ref[0, :, hh * 256 + NOPE:(hh + 1) * 256] = kr
        vb_ref[0, :, hh * 256:hh * 256 + VB] = kv[:, base + NOPE:base + NOPE + VB].astype(BF16)
        vb_ref[0, :, hh * 256 + VB:(hh + 1) * 256] = ones_col

    @pl.when(i >= nct)
    def _():
        qa = proj(c_qa, c_ql)
        for b in range(qa.shape[1] // LANES):
            sl = slice(b * LANES, (b + 1) * LANES)
            qa_ref[0, :, sl] = (_rope_block(qa[:, sl], cf, sf) * (DA ** -0.5)).astype(BF16)
        ql = _rms(proj(c_ql, c_g), qg_ref[...]).astype(BF16)
        qb = jnp.dot(ql, wq_ref[...], preferred_element_type=F32)
        for hh in range(HB):
            qm_ref[0, :, hh * 256:hh * 256 + NOPE] = qb[:, hh * 256:hh * 256 + NOPE].astype(BF16)
            qm_ref[0, :, hh * 256 + NOPE:(hh + 1) * 256] = _rope_block(
                qb[:, hh * 256 + NOPE:(hh + 1) * 256], ch, shh).astype(BF16)
        gs_ref[0] = jax.nn.sigmoid(proj(c_g, c_ka)).astype(BF16)


def _projection(x, ctx, sc1, sh1, w_all, q_norm_g, kv_norm_g, wq, wkv, cf, sf, ch, shh):
    bsz, n, d = x.shape
    nctx = ctx.shape[1]
    tn = TOKEN_TILE
    nct = nctx // tn
    s = nctx + n
    ncols = w_all.shape[1]

    def lat(b, i):
        return (b, jnp.maximum(i - nct, 0), 0)

    def cx(b, i):
        return (b, jnp.minimum(i, nct - 1), 0)

    def modrow(b, i):
        return (jnp.where(i < nct, bsz, b), 0, 0)

    def allk(b, i):
        return (b, i, 0)

    def tab(b, i):
        return (i, 0)

    bf = lambda w: jax.ShapeDtypeStruct((bsz, n, w), BF16)
    bfs = lambda w: jax.ShapeDtypeStruct((bsz, s, w), BF16)
    return pl.pallas_call(
        functools.partial(_proj_kernel, nct),
        out_shape=(bf(HA * 2 * DA), bf(HB * 256), bf(2 * d), bfs(HA * 2 * DA), bfs(HA * 256), bfs(HB * 256), bfs(HB * 256)),
        grid=(bsz, s // tn),
        in_specs=[
            pl.BlockSpec((1, tn, d), lat), pl.BlockSpec((1, tn, d), cx),
            pl.BlockSpec((1, 1, d), modrow), pl.BlockSpec((1, 1, d), modrow),
            _resident((d, ncols)), _full((1, Q_LORA)), _full((1, KV_LORA)),
            _resident(wq.shape), _resident(wkv.shape),
            pl.BlockSpec((tn, LANES), tab), pl.BlockSpec((tn, LANES), tab),
            pl.BlockSpec((tn, LANES), tab), pl.BlockSpec((tn, LANES), tab),
        ],
        out_specs=(
            pl.BlockSpec((1, tn, HA * 2 * DA), lat), pl.BlockSpec((1, tn, HB * 256), lat),
            pl.BlockSpec((1, tn, 2 * d), lat),
            pl.BlockSpec((1, tn, HA * 2 * DA), allk), pl.BlockSpec((1, tn, HA * 256), allk),
            pl.BlockSpec((1, tn, HB * 256), allk), pl.BlockSpec((1, tn, HB * 256), allk),
        ),
        compiler_params=_cparams(("parallel", "arbitrary")),
        name="projection",
    )(x, ctx, sc1, sh1, w_all, q_norm_g, kv_norm_g, wq, wkv, cf, sf, ch, shh)


_NT = (((1,), (1,)), ((), ()))


def _softmax_pv(q, k, v_ones, scale):
    s = lax.dot_general(q, k, _NT, preferred_element_type=F32)
    m = jnp.max(s, axis=-1, keepdims=True)
    p = jnp.exp((s - m) * scale).astype(BF16)
    o = jnp.dot(p, v_ones, preferred_element_type=F32)
    w = o.shape[1] // 2
    return o[:, :w] / o[:, w:w + 1]


def _mixers_kernel(qa_ref, ka_ref, va_ref, qm_ref, km_ref, vb_ref, lam_ref, g_ref, ya_ref, yb_ref):
    q = qa_ref[0]
    k = ka_ref[0]
    v = va_ref[0]
    lane = lax.broadcasted_iota(jnp.int32, q.shape, 1)
    zero = jnp.zeros_like(q)
    o1 = _softmax_pv(jnp.where(lane < DA, q, zero), k, v, 1.0)
    o2 = _softmax_pv(jnp.where(lane < DA, zero, q), k, v, 1.0)
    lv = lam_ref[...]
    lam = (jnp.exp(jnp.sum(lv[0:1] * lv[1:2], axis=-1, keepdims=True))
           - jnp.exp(jnp.sum(lv[2:3] * lv[3:4], axis=-1, keepdims=True)) + LAMBDA_INIT)
    o = o1 - lam * o2
    y = o * lax.rsqrt(jnp.mean(o * o, axis=-1, keepdims=True) + EPS) * g_ref[...]
    ya_ref[0] = (y * (1.0 - LAMBDA_INIT)).astype(BF16)
    yb_ref[0] = _softmax_pv(qm_ref[0], km_ref[0], vb_ref[0], (NOPE + ROPE) ** -0.5).astype(BF16)


def _mixers(qa, ka, va, qm, km, vb, lam_rows, subln_g):
    assert HA == HB
    bsz, n, w = qa.shape
    s = ka.shape[1]
    tq = Q_TILE
    qspec = lambda width: pl.BlockSpec((1, tq, width), lambda b, h, i: (b, i, h))
    kspec = lambda width: pl.BlockSpec((1, s, width), lambda b, h, i: (b, 0, h))
    return pl.pallas_call(
        _mixers_kernel,
        out_shape=(jax.ShapeDtypeStruct((bsz, n, w), BF16), jax.ShapeDtypeStruct((bsz, n, HB * VB), BF16)),
        grid=(bsz, HA, n // tq),
        in_specs=[qspec(LANES), kspec(LANES), kspec(256), qspec(256), kspec(256), kspec(256),
                  _full(lam_rows.shape), _full(subln_g.shape)],
        out_specs=(qspec(LANES), qspec(VB)),
        compiler_params=_cparams(("parallel", "parallel", "arbitrary")),
        name="token_mixers",
    )(qa, ka, va, qm, km, vb, lam_rows, subln_g)


def _layer_norm(r, g, b):
    mu = jnp.mean(r, axis=-1, keepdims=True)
    c = r - mu
    var = jnp.mean(c * c, axis=-1, keepdims=True)
    return c * lax.rsqrt(var + EPS) * g + b


def _merge_kernel(ya_ref, yb_ref, gs_ref, x_ref, wpa_ref, wpb_ref, wo_ref, gt1_ref, sc2_ref, sh2_ref,
                  g1_ref, b1_ref, wpq_ref, keys_ref, x1_ref, h2_ref, sct_ref):
    d = x_ref.shape[-1]
    gs = gs_ref[...].astype(F32)
    za = jnp.dot(ya_ref[...], wpa_ref[...], preferred_element_type=F32)
    zb = jnp.dot(yb_ref[...], wpb_ref[...], preferred_element_type=F32)
    z = gs[:, :d] * za + gs[:, d:] * zb
    y = jnp.dot(z.astype(BF16), wo_ref[...], preferred_element_type=F32)
    x1 = _layer_norm(DEEPNORM_ALPHA * x_ref[...] + gt1_ref[0] * y, g1_ref[...], b1_ref[...])
    x1_ref[...] = x1
    h2 = x1 * (1.0 + sc2_ref[0]) + sh2_ref[0]
    h2_ref[...] = h2
    qp = jnp.dot(h2.astype(BF16), wpq_ref[...], preferred_element_type=F32).astype(BF16)
    for hh in range(2 * HP):
        sct_ref[hh] = lax.dot_general(keys_ref[hh], qp[:, hh * DK_HALF:(hh + 1) * DK_HALF], _NT,
                                      preferred_element_type=F32)


def _merge(ya, yb, gs, x2d, wpa, wpb, wo, gt1, sc2, sh2, ln1_g, ln1_b, wpq, keys, tiles_per_batch):
    t, d = x2d.shape
    tn = TOKEN_TILE
    row = lambda i: (i, 0)
    brow = lambda i: (i // tiles_per_batch, 0, 0)
    return pl.pallas_call(
        _merge_kernel,
        out_shape=(jax.ShapeDtypeStruct((t, d), F32), jax.ShapeDtypeStruct((t, d), F32),
                   jax.ShapeDtypeStruct((2 * HP, N_KEYS, t), F32)),
        grid=(t // tn,),
        in_specs=[
            pl.BlockSpec((tn, d), row), pl.BlockSpec((tn, d), row), pl.BlockSpec((tn, 2 * d), row),
            pl.BlockSpec((tn, d), row),
            _resident(wpa.shape), _resident(wpb.shape), _resident(wo.shape),
            pl.BlockSpec((1, 1, d), brow), pl.BlockSpec((1, 1, d), brow), pl.BlockSpec((1, 1, d), brow),
            _full((1, d)), _full((1, d)), _resident(wpq.shape), _resident(keys.shape),
        ],
        out_specs=(pl.BlockSpec((tn, d), row), pl.BlockSpec((tn, d), row),
                   pl.BlockSpec((2 * HP, N_KEYS, tn), lambda i: (0, 0, i))),
        compiler_params=_cparams(("parallel",)),
        name="merge",
    )(ya, yb, gs, x2d, wpa, wpb, wo, gt1, sc2, sh2, ln1_g, ln1_b, wpq, keys)


SUBLANES = 8


def _tree(items, merge):
    while len(items) > 1:
        nxt = [merge(items[k], items[k + 1]) for k in range(0, len(items) - 1, 2)]
        if len(items) % 2:
            nxt.append(items[-1])
        items = nxt
    return items[0]


def _all_sublanes(x, op):
    for sh in (4, 2, 1):
        x = op(x, pltpu.roll(x, sh, axis=0))
    return x


def _top_keys(s_ref, k):
    nblk = s_ref.shape[0] // SUBLANES
    tl = s_ref.shape[1]
    sub = lax.broadcasted_iota(jnp.int32, (SUBLANES, tl), 0)
    rows = [sub + SUBLANES * b for b in range(nblk)]
    vs = [s_ref[SUBLANES * b:SUBLANES * (b + 1), :] for b in range(nblk)]

    def merge(a, b):
        c = a[0] >= b[0]
        return jnp.where(c, a[0], b[0]), jnp.where(c, a[1], b[1])

    vals, idxs = [], []
    for _ in range(k):
        v, i = _tree(list(zip(vs, rows)), merge)
        for sh in (4, 2, 1):
            pv, pi = pltpu.roll(v, sh, axis=0), pltpu.roll(i, sh, axis=0)
            c = (v > pv) | ((v == pv) & (i < pi))
            v, i = jnp.where(c, v, pv), jnp.where(c, i, pi)
        vals.append(v[0:1])
        idxs.append(i[0:1])
        vs = [jnp.where(rows[b] == i, -jnp.inf, vs[b]) for b in range(nblk)]
    return jnp.concatenate(vals, axis=0), jnp.concatenate(idxs, axis=0)


_CAND_BLOCKS = (("col", 0, 0), ("col", 0, 8), ("col", 1, 0),
                ("row", 0, 0, 2, 7), ("row", 0, 8, 8, 15), ("row", 1, 0, 2, 7),
                ("row", 2, 0, 2, 4), ("row", 3, 0, 2, 3), ("row", 4, 0, 2, 2))
assert TOPK == 16 and TOPK_HALF == 16 and sum(
    SUBLANES if blk[0] == "col" else blk[4] - blk[3] + 1 for blk in _CAND_BLOCKS) == sum(
    1 for a in range(16) for b in range(16) if (a + 1) * (b + 1) <= TOPK)


def _top_pairs(v1, i1, v2, i2, k):
    tl = v1.shape[1]
    sub = lax.broadcasted_iota(jnp.int32, (SUBLANES, tl), 0)
    big = TOPK_HALF * TOPK_HALF
    cs, flats, pays = [], [], []
    for blk in _CAND_BLOCKS:
        if blk[0] == "col":
            _, b, a0 = blk
            cs.append(v1[a0:a0 + SUBLANES] + v2[b:b + 1])
            pays.append((i1[a0:a0 + SUBLANES] * N_KEYS + i2[b:b + 1]) * SLAB_ROWS)
            flats.append((sub + a0) * TOPK_HALF + b)
        else:
            _, a, b0, lo, hi = blk
            ok = (sub + b0 >= lo) & (sub + b0 <= hi)
            cs.append(jnp.where(ok, v1[a:a + 1] + v2[b0:b0 + SUBLANES], -jnp.inf))
            pays.append((i1[a:a + 1] * N_KEYS + i2[b0:b0 + SUBLANES]) * SLAB_ROWS)
            flats.append(jnp.where(ok, a * TOPK_HALF + sub + b0, big))
    vals, outs = [], []
    for _ in range(k):
        m = _all_sublanes(_tree(cs, jnp.maximum), jnp.maximum)
        pos = _all_sublanes(_tree([jnp.where(c == m, f, big) for c, f in zip(cs, flats)], jnp.minimum), jnp.minimum)
        sels = [f == pos for f in flats]
        pay = _all_sublanes(_tree([jnp.where(s, p, -1) for s, p in zip(sels, pays)], jnp.maximum), jnp.maximum)
        vals.append(m[0:1])
        outs.append(pay[0:1])
        cs = [jnp.where(s, -jnp.inf, c) for s, c in zip(sels, cs)]
    return jnp.concatenate(vals, axis=0), jnp.concatenate(outs, axis=0)


def _route_kernel(sct_ref, idx_ref, g_ref, idxt_ref):
    def head(hh, carry):
        v1, i1 = _top_keys(sct_ref.at[2 * hh], TOPK_HALF)
        v2, i2 = _top_keys(sct_ref.at[2 * hh + 1], TOPK_HALF)
        gv, gi = _top_pairs(v1, i1, v2, i2, TOPK)
        e = jnp.exp(gv - gv[0:1])
        rows = pl.ds(pl.multiple_of(hh * TOPK, TOPK), TOPK)
        g_ref[rows, :] = e / jnp.sum(e, axis=0, keepdims=True)
        idxt_ref[rows, :] = gi.astype(F32)
        return carry

    lax.fori_loop(0, HP, head, 0)
    idx_ref[...] = jnp.transpose(idxt_ref[...]).astype(jnp.int32)


def _routing(sct):
    _, _, t = sct.shape
    tl = ROUTE_TILE
    nsel = HP * TOPK
    return pl.pallas_call(
        _route_kernel,
        out_shape=(jax.ShapeDtypeStruct((t, nsel), jnp.int32), jax.ShapeDtypeStruct((nsel, t), F32)),
        grid=(t // tl,),
        in_specs=[pl.BlockSpec((2 * HP, N_KEYS, tl), lambda i: (0, 0, i))],
        out_specs=(pl.BlockSpec((tl, nsel), lambda i: (i, 0)), pl.BlockSpec((nsel, tl), lambda i: (0, i))),
        scratch_shapes=[pltpu.VMEM((nsel, tl), F32)],
        compiler_params=_cparams(("parallel",)),
        name="routing",
    )(sct)


SLAB_ROWS = 4
_HI_MASK = 0xFFFF0000


def _unpack(slab):
    lo = lax.bitcast_convert_type(slab << 16, F32)
    hi = lax.bitcast_convert_type(slab & jnp.uint32(_HI_MASK), F32)
    return lo, hi


def _gather_slabs(idx_ref, tab_ref, slab_ref, t):
    group = 16
    for j0 in range(0, idx_ref.shape[1], group):
        row = idx_ref.at[t, pl.ds(j0, group)]
        for k in range(group):
            j = j0 + k
            first = pl.multiple_of(row[k], SLAB_ROWS)
            slab_ref[SLAB_ROWS * j:SLAB_ROWS * (j + 1), :] = tab_ref[pl.ds(first, SLAB_ROWS), :]


def _two_phase_loop(tb, gather, math, init):
    gather(0, 0)
    gather(1, 1)

    def two_tokens(i, carry):
        t0 = 2 * i
        carry = math(0, t0, carry)
        carry = math(1, t0 + 1, carry)
        gather(0, jnp.minimum(t0 + 2, tb - 1))
        gather(1, jnp.minimum(t0 + 3, tb - 1))
        return carry

    return lax.fori_loop(0, tb // 2, two_tokens, init)


def _slab_rows(slab_ref, s, nsel):
    return slab_ref[pl.ds(s, nsel, stride=4), :]


def _expert_down_kernel(idx_ref, x_ref, g_ref, u_ref, w_ref, slab0_ref, slab1_ref):
    tb = x_ref.shape[0]
    nsel = idx_ref.shape[1]
    slabs = (slab0_ref, slab1_ref)
    lane = lax.broadcasted_iota(jnp.int32, (nsel, tb), 1)

    def gather(buf, t):
        _gather_slabs(idx_ref, u_ref, slabs[buf], t)

    def math(buf, t, at):
        xv = x_ref[t]
        acc = None
        for s in range(4):
            lo, hi = _unpack(_slab_rows(slabs[buf], s, nsel))
            term = lo * xv[s:s + 1] + hi * xv[4 + s:5 + s]
            acc = term if acc is None else acc + term
        a = jnp.sum(acc, axis=1, keepdims=True)
        return jnp.where(lane == t, a, at)

    at = _two_phase_loop(tb, gather, math, jnp.zeros((nsel, tb), F32))
    w_ref[...] = jax.nn.gelu(at) * g_ref[...]


def _expert_down(idx, x3, g_t, u_tab):
    t = x3.shape[0]
    tb = EXPERT_TILE
    nsel = idx.shape[1]
    return pl.pallas_call(
        _expert_down_kernel,
        out_shape=jax.ShapeDtypeStruct((nsel, t), F32),
        grid=(t // tb,),
        in_specs=[
            pl.BlockSpec((tb, nsel), lambda i: (i, 0), memory_space=pltpu.SMEM),
            pl.BlockSpec((tb, 8, LANES), lambda i: (i, 0, 0)),
            pl.BlockSpec((nsel, tb), lambda i: (0, i)),
            _resident(u_tab.shape),
        ],
        out_specs=pl.BlockSpec((nsel, tb), lambda i: (0, i)),
        scratch_shapes=[pltpu.VMEM((4 * nsel, LANES), jnp.uint32), pltpu.VMEM((4 * nsel, LANES), jnp.uint32)],
        compiler_params=_cparams(("arbitrary",)),
        name="expert_down",
    )(idx, x3, g_t, u_tab)


def _expert_up_kernel(idx_ref, wt_ref, x1_ref, v_ref, gt2_ref, g2_ref, b2_ref, o_ref, acc_ref, slab0_ref, slab1_ref):
    tb = x1_ref.shape[0]
    nsel = idx_ref.shape[1]
    slabs = (slab0_ref, slab1_ref)

    def gather(buf, t):
        _gather_slabs(idx_ref, v_ref, slabs[buf], t)

    def math(buf, t, carry):
        wt = wt_ref[...]
        wb = jnp.take_along_axis(wt, jnp.full(wt.shape, t, jnp.int32), axis=1, mode="promise_in_bounds")
        los, his = [], []
        for s in range(4):
            lo, hi = _unpack(_slab_rows(slabs[buf], s, nsel))
            los.append(jnp.sum(lo * wb, axis=0, keepdims=True))
            his.append(jnp.sum(hi * wb, axis=0, keepdims=True))
        acc_ref[t] = jnp.concatenate(los + his, axis=0)
        return carry

    _two_phase_loop(tb, gather, math, 0)
    r = DEEPNORM_ALPHA * x1_ref[...] + gt2_ref[...] * acc_ref[...]
    n = r.shape[1] * r.shape[2]
    mu = jnp.sum(jnp.sum(r, axis=2, keepdims=True), axis=1, keepdims=True) / n
    c = r - mu
    var = jnp.sum(jnp.sum(c * c, axis=2, keepdims=True), axis=1, keepdims=True) / n
    o_ref[...] = c * lax.rsqrt(var + EPS) * g2_ref[...] + b2_ref[...]


def _expert_up(idx, w_t, x13, v_tab, gt2, ln2_g, ln2_b, tiles_per_batch):
    t = x13.shape[0]
    tb = EXPERT_TILE
    nsel = idx.shape[1]
    tok = pl.BlockSpec((tb, 8, LANES), lambda i: (i, 0, 0))
    return pl.pallas_call(
        _expert_up_kernel,
        out_shape=jax.ShapeDtypeStruct(x13.shape, F32),
        grid=(t // tb,),
        in_specs=[pl.BlockSpec((tb, nsel), lambda i: (i, 0), memory_space=pltpu.SMEM),
                  pl.BlockSpec((nsel, tb), lambda i: (0, i)), tok, _resident(v_tab.shape),
                  pl.BlockSpec((1, 8, LANES), lambda i: (i // tiles_per_batch, 0, 0)),
                  _full((1, 8, LANES)), _full((1, 8, LANES))],
        out_specs=tok,
        scratch_shapes=[pltpu.VMEM((tb, 8, LANES), F32),
                        pltpu.VMEM((4 * nsel, LANES), jnp.uint32), pltpu.VMEM((4 * nsel, LANES), jnp.uint32)],
        compiler_params=_cparams(("arbitrary",)),
        name="expert_up",
    )(idx, w_t, x13, v_tab, gt2, ln2_g, ln2_b)


def _pack_table(tab):
    e, d = tab.shape
    bits = lax.bitcast_convert_type(tab.astype(BF16), jnp.uint16).astype(jnp.uint32)
    half = d // 2
    word = bits[:, :half] | (bits[:, half:] << 16)
    assert half == SLAB_ROWS * LANES
    return word.reshape(e * SLAB_ROWS, LANES)


def _rope_tables(n, nctx):
    rows = n // GRID_W
    quarter = DA // 4
    inv = ROPE_BASE ** (-jnp.arange(quarter, dtype=F32) / quarter)
    row_ang = jnp.arange(rows, dtype=F32)[:, None] * inv
    col_ang = jnp.arange(GRID_W, dtype=F32)[:, None] * inv
    ang = jnp.concatenate([
        jnp.broadcast_to(row_ang[:, None, :], (rows, GRID_W, quarter)),
        jnp.broadcast_to(col_ang[None, :, :], (rows, GRID_W, quarter))], -1).reshape(n, 2 * quarter)
    cos = jnp.concatenate([jnp.ones((nctx, 2 * quarter), F32), jnp.cos(ang)], 0)
    sin = jnp.concatenate([jnp.zeros((nctx, 2 * quarter), F32), jnp.sin(ang)], 0)
    one, zero = jnp.ones_like(cos), jnp.zeros_like(sin)
    cf = jnp.concatenate([cos, cos, cos, cos], 1)
    sf = jnp.concatenate([-sin, sin, -sin, sin], 1)
    ch = jnp.concatenate([cos, cos, one, one], 1)
    sh = jnp.concatenate([-sin, sin, zero, zero], 1)
    return cf, sf, ch, sh


def kernel(x, c, ctx, c_ctx, w_mod, b_mod, w_in, w_q_up, q_norm_g, w_kv_up, kv_norm_g, lambda_q1, lambda_k1,
           lambda_q2, lambda_k2, subln_g, w_pa, w_pb, w_out, ln1_g, ln1_b, w_pq, peer_keys, peer_u, peer_v,
           ln2_g, ln2_b):
    bsz, n, d = x.shape
    nctx = ctx.shape[1]
    t = bsz * n
    assert d == 8 * LANES and n % TOKEN_TILE == 0 and nctx % TOKEN_TILE == 0 and n % GRID_W == 0
    assert t % ROUTE_TILE == 0 and n % EXPERT_TILE == 0 and w_mod.shape[0] == DEPTH

    rows = -(-(bsz + 1) // 8) * 8
    c_rows = jnp.concatenate([c, c_ctx[None, :], jnp.zeros((rows - bsz - 1, d), F32)], 0)
    mod = _modulation(c_rows, w_mod[0], b_mod[0][None, :])[:bsz + 1]
    sh1, sc1, gt1, sh2, sc2, gt2 = [m[:, None, :] for m in jnp.split(mod, 6, axis=-1)]

    w_all = jnp.concatenate([w_in[0], jnp.zeros((d, LANES - ROPE), F32)], 1).astype(BF16)
    wq = w_q_up[0].reshape(Q_LORA, HB, NOPE + ROPE)
    wq = jnp.concatenate([wq, jnp.zeros((Q_LORA, HB, 256 - NOPE - ROPE), F32)], -1).reshape(Q_LORA, HB * 256)
    cf, sf, ch, shh = _rope_tables(n, nctx)
    lam_rows = jnp.zeros((8, LANES), F32).at[0:4, 0:DA].set(
        jnp.stack([lambda_q1[0], lambda_k1[0], lambda_q2[0], lambda_k2[0]]))

    qa, qm, gs, ka, va, km, vb = _projection(
        x, ctx, sc1, sh1, w_all, q_norm_g[0][None, :], kv_norm_g[0][None, :], wq.astype(BF16),
        w_kv_up[0].astype(BF16), cf, sf, ch, shh)

    ya, yb = _mixers(qa, ka, va, qm, km, vb, lam_rows, subln_g[0][None, :])

    keys = peer_keys[0].reshape(2 * HP, N_KEYS, DK_HALF).astype(BF16)
    x1, h2, sct = _merge(
        ya.reshape(t, d), yb.reshape(t, d), gs.reshape(t, 2 * d), x.reshape(t, d),
        w_pa[0].astype(BF16), w_pb[0].astype(BF16), w_out[0].astype(BF16),
        gt1[:bsz], sc2[:bsz], sh2[:bsz], ln1_g[0][None, :], ln1_b[0][None, :],
        w_pq[0].astype(BF16), keys, n // TOKEN_TILE)

    idx, g_t = _routing(sct)

    w_t = _expert_down(idx, h2.reshape(t, 8, LANES), g_t, _pack_table(peer_u[0]))
    out = _expert_up(idx, w_t, x1.reshape(t, 8, LANES), _pack_table(peer_v[0]),
                     gt2[:bsz].reshape(bsz, 8, LANES), ln2_g[0].reshape(1, 8, LANES),
                     ln2_b[0].reshape(1, 8, LANES), n // EXPERT_TILE)
    return out.reshape(bsz, n, d)
```

```python
import functools
import math

import jax
import jax.numpy as jnp
from jax import lax
from jax.experimental import pallas as pl
from jax.experimental.pallas import tpu as pltpu

GRID_W = 64
HA, DA = 8, 64
HB, NOPE, ROPE, VB = 8, 128, 64, 128
Q_LORA, KV_LORA = 256, 256
N_KEYS = 128
HP, DK_HALF = 8, 128
TOPK_HALF, TOPK = 16, 16
ROPE_BASE = 10000.0
EPS = 1e-6
DEPTH = 1
DEEPNORM_ALPHA = (2 * DEPTH) ** 0.25
LAMBDA_INIT = 0.8 - 0.6 * math.exp(-0.3 * 0)
LOG2E = 1.0 / math.log(2.0)
QA_SCALE = DA ** -0.5 * LOG2E
QM_SCALE = (NOPE + ROPE) ** -0.5 * LOG2E

LANES = 128
VMEM_LIMIT_BYTES = 56 * 1024 * 1024

TOKEN_TILE = 256
Q_TILE = 512
ROUTE_TILE = 512
EXPERT_TILE = 128

F32 = jnp.float32
BF16 = jnp.bfloat16


def _cparams(sem):
    return pltpu.CompilerParams(dimension_semantics=sem, vmem_limit_bytes=VMEM_LIMIT_BYTES)


def _full(shape):
    n = len(shape)
    return pl.BlockSpec(shape, lambda *_: (0,) * n)


def _resident(shape):
    n = len(shape)
    return pl.BlockSpec(shape, lambda *_: (0,) * n, pipeline_mode=pl.Buffered(1))


def _mod_kernel(c_ref, w_ref, b_ref, o_ref):
    c = c_ref[...]
    s = c * jax.nn.sigmoid(c)
    o_ref[...] = jnp.dot(s, w_ref[...], preferred_element_type=F32) + b_ref[...]


def _modulation(c_rows, w_mod, b_mod):
    r, d = c_rows.shape
    n = w_mod.shape[1]
    tn = 1536
    return pl.pallas_call(
        _mod_kernel,
        out_shape=jax.ShapeDtypeStruct((r, n), F32),
        grid=(n // tn,),
        in_specs=[_full((r, d)), pl.BlockSpec((d, tn), lambda j: (0, j)), pl.BlockSpec((1, tn), lambda j: (0, j))],
        out_specs=pl.BlockSpec((r, tn), lambda j: (0, j)),
        compiler_params=_cparams(("arbitrary",)),
        name="modulation",
    )(c_rows, w_mod, b_mod)


def _swap_halves(blk):
    lane = lax.broadcasted_iota(jnp.int32, blk.shape, 1)
    up = pltpu.roll(blk, LANES - 32, axis=1)
    dn = pltpu.roll(blk, 32, axis=1)
    return jnp.where((lane % 64) < 32, up, dn)


def _rope_block(blk, c, s):
    return blk * c + _swap_halves(blk) * s


def _rms(x, g):
    return x * lax.rsqrt(jnp.mean(x * x, axis=-1, keepdims=True) + EPS) * g


def _proj_kernel(nct, x_ref, ctx_ref, sc_ref, sh_ref, w_ref, qg_ref, kvg_ref, wq_ref, wkv_ref,
                 cf_ref, sf_ref, ch_ref, shh_ref,
                 qa_ref, qm_ref, gs_ref, ka_ref, va_ref, km_ref, vb_ref):
    i = pl.program_id(1)
    d = x_ref.shape[-1]
    xt = jnp.where(i < nct, ctx_ref[0], x_ref[0])
    h = xt * (1.0 + sc_ref[0]) + sh_ref[0]
    hb = h.astype(BF16)
    cf, sf = cf_ref[...], sf_ref[...]
    ch, shh = ch_ref[...], shh_ref[...]

    def proj(lo, hi):
        return jnp.dot(hb, w_ref[:, lo:hi], preferred_element_type=F32)

    c_qa, c_ql, c_g = 0, HA * 2 * DA, HA * 2 * DA + Q_LORA
    c_ka = c_g + 2 * d
    c_va = c_ka + HA * 2 * DA
    c_kvl = c_va + HA * 2 * DA
    c_kr = c_kvl + KV_LORA

    ka = proj(c_ka, c_va)
    for b in range(ka.shape[1] // LANES):
        sl = slice(b * LANES, (b + 1) * LANES)
        ka_ref[0, :, sl] = _rope_block(ka[:, sl], cf, sf).astype(BF16)
    lane = lax.broadcasted_iota(jnp.int32, (hb.shape[0], LANES), 1)
    ones_col = jnp.where(lane == 0, 1.0, 0.0).astype(BF16)
    va = proj(c_va, c_kvl).astype(BF16)
    for hh in range(HA):
        va_ref[0, :, hh * 256:hh * 256 + 2 * DA] = va[:, hh * 2 * DA:(hh + 1) * 2 * DA]
        va_ref[0, :, hh * 256 + 2 * DA:(hh + 1) * 256] = ones_col
    kvl = _rms(proj(c_kvl, c_kr), kvg_ref[...]).astype(BF16)
    kv = jnp.dot(kvl, wkv_ref[...], preferred_element_type=F32)
    kr = _rope_block(proj(c_kr, c_kr + LANES), ch, shh).astype(BF16)
    for hh in range(HB):
        base = hh * (NOPE + VB)
        km_ref[0, :, hh * 256:hh * 256 + NOPE] = kv[:, base:base + NOPE].astype(BF16)
        km_ref[0, :, hh * 256 + NOPE:(hh + 1) * 256] = kr
        vb_ref[0, :, hh * 256:hh * 256 + VB] = kv[:, base + NOPE:base + NOPE + VB].astype(BF16)
        vb_ref[0, :, hh * 256 + VB:(hh + 1) * 256] = ones_col

    @pl.when(i >= nct)
    def _():
        qa = proj(c_qa, c_ql)
        for b in range(qa.shape[1] // LANES):
            sl = slice(b * LANES, (b + 1) * LANES)
            qa_ref[0, :, sl] = (_rope_block(qa[:, sl], cf, sf) * QA_SCALE).astype(BF16)
        ql = _rms(proj(c_ql, c_g), qg_ref[...]).astype(BF16)
        qb = jnp.dot(ql, wq_ref[...], preferred_element_type=F32) * QM_SCALE
        for hh in range(HB):
            qm_ref[0, :, hh * 256:hh * 256 + NOPE] = qb[:, hh * 256:hh * 256 + NOPE].astype(BF16)
            qm_ref[0, :, hh * 256 + NOPE:(hh + 1) * 256] = _rope_block(
                qb[:, hh * 256 + NOPE:(hh + 1) * 256], ch, shh).astype(BF16)
        gs_ref[0] = jax.nn.sigmoid(proj(c_g, c_ka)).astype(BF16)


def _projection(x, ctx, sc1, sh1, w_all, q_norm_g, kv_norm_g, wq, wkv, cf, sf, ch, shh):
    bsz, n, d = x.shape
    nctx = ctx.shape[1]
    tn = TOKEN_TILE
    nct = nctx // tn
    s = nctx + n
    ncols = w_all.shape[1]

    def lat(b, i):
        return (b, jnp.maximum(i - nct, 0), 0)

    def cx(b, i):
        return (b, jnp.minimum(i, nct - 1), 0)

    def modrow(b, i):
        return (jnp.where(i < nct, bsz, b), 0, 0)

    def allk(b, i):
        return (b, i, 0)

    def tab(b, i):
        return (i, 0)

    bf = lambda w: jax.ShapeDtypeStruct((bsz, n, w), BF16)
    bfs = lambda w: jax.ShapeDtypeStruct((bsz, s, w), BF16)
    return pl.pallas_call(
        functools.partial(_proj_kernel, nct),
        out_shape=(bf(HA * 2 * DA), bf(HB * 256), bf(2 * d), bfs(HA * 2 * DA), bfs(HA * 256), bfs(HB * 256), bfs(HB * 256)),
        grid=(bsz, s // tn),
        in_specs=[
            pl.BlockSpec((1, tn, d), lat), pl.BlockSpec((1, tn, d), cx),
            pl.BlockSpec((1, 1, d), modrow), pl.BlockSpec((1, 1, d), modrow),
            _resident((d, ncols)), _full((1, Q_LORA)), _full((1, KV_LORA)),
            _resident(wq.shape), _resident(wkv.shape),
            pl.BlockSpec((tn, LANES), tab), pl.BlockSpec((tn, LANES), tab),
            pl.BlockSpec((tn, LANES), tab), pl.BlockSpec((tn, LANES), tab),
        ],
        out_specs=(
            pl.BlockSpec((1, tn, HA * 2 * DA), lat), pl.BlockSpec((1, tn, HB * 256), lat),
            pl.BlockSpec((1, tn, 2 * d), lat),
            pl.BlockSpec((1, tn, HA * 2 * DA), allk), pl.BlockSpec((1, tn, HA * 256), allk),
            pl.BlockSpec((1, tn, HB * 256), allk), pl.BlockSpec((1, tn, HB * 256), allk),
        ),
        compiler_params=_cparams(("parallel", "arbitrary")),
        name="projection",
    )(x, ctx, sc1, sh1, w_all, q_norm_g, kv_norm_g, wq, wkv, cf, sf, ch, shh)


_NT = (((1,), (1,)), ((), ()))


def _softmax_pv(q, k, v_ones):
    s = lax.dot_general(q, k, _NT, preferred_element_type=F32)
    m = jnp.max(s, axis=-1, keepdims=True)
    p = jnp.exp2(s - m).astype(BF16)
    o = jnp.dot(p, v_ones, preferred_element_type=F32)
    w = o.shape[1] // 2
    return o[:, :w] / o[:, w:w + 1]


def _mixers_kernel(qa_ref, ka_ref, va_ref, qm_ref, km_ref, vb_ref, lam_ref, g_ref, ya_ref, yb_ref):
    q = qa_ref[0]
    k = ka_ref[0]
    v = va_ref[0]
    lane = lax.broadcasted_iota(jnp.int32, q.shape, 1)
    zero = jnp.zeros_like(q)
    o1 = _softmax_pv(jnp.where(lane < DA, q, zero), k, v)
    o2 = _softmax_pv(jnp.where(lane < DA, zero, q), k, v)
    lv = lam_ref[...]
    lam = (jnp.exp(jnp.sum(lv[0:1] * lv[1:2], axis=-1, keepdims=True))
           - jnp.exp(jnp.sum(lv[2:3] * lv[3:4], axis=-1, keepdims=True)) + LAMBDA_INIT)
    o = o1 - lam * o2
    y = o * lax.rsqrt(jnp.mean(o * o, axis=-1, keepdims=True) + EPS) * g_ref[...]
    ya_ref[0] = (y * (1.0 - LAMBDA_INIT)).astype(BF16)
    yb_ref[0] = _softmax_pv(qm_ref[0], km_ref[0], vb_ref[0]).astype(BF16)


def _mixers(qa, ka, va, qm, km, vb, lam_rows, subln_g):
    assert HA == HB
    bsz, n, w = qa.shape
    s = ka.shape[1]
    tq = Q_TILE
    qspec = lambda width: pl.BlockSpec((1, tq, width), lambda b, h, i: (b, i, h))
    kspec = lambda width: pl.BlockSpec((1, s, width), lambda b, h, i: (b, 0, h))
    return pl.pallas_call(
        _mixers_kernel,
        out_shape=(jax.ShapeDtypeStruct((bsz, n, w), BF16), jax.ShapeDtypeStruct((bsz, n, HB * VB), BF16)),
        grid=(bsz, HA, n // tq),
        in_specs=[qspec(LANES), kspec(LANES), kspec(256), qspec(256), kspec(256), kspec(256),
                  _full(lam_rows.shape), _full(subln_g.shape)],
        out_specs=(qspec(LANES), qspec(VB)),
        compiler_params=_cparams(("parallel", "parallel", "arbitrary")),
        name="token_mixers",
    )(qa, ka, va, qm, km, vb, lam_rows, subln_g)


def _layer_norm(r, g, b):
    mu = jnp.mean(r, axis=-1, keepdims=True)
    c = r - mu
    var = jnp.mean(c * c, axis=-1, keepdims=True)
    return c * lax.rsqrt(var + EPS) * g + b


def _merge_kernel(ya_ref, yb_ref, gs_ref, x_ref, wpa_ref, wpb_ref, wo_ref, gt1_ref, sc2_ref, sh2_ref,
                  g1_ref, b1_ref, wpq_ref, keys_ref, x1_ref, h2_ref, sct_ref):
    d = x_ref.shape[-1]
    gs = gs_ref[...].astype(F32)
    za = jnp.dot(ya_ref[...], wpa_ref[...], preferred_element_type=F32)
    zb = jnp.dot(yb_ref[...], wpb_ref[...], preferred_element_type=F32)
    z = gs[:, :d] * za + gs[:, d:] * zb
    y = jnp.dot(z.astype(BF16), wo_ref[...], preferred_element_type=F32)
    x1 = _layer_norm(DEEPNORM_ALPHA * x_ref[...] + gt1_ref[0] * y, g1_ref[...], b1_ref[...])
    x1_ref[...] = x1
    h2 = x1 * (1.0 + sc2_ref[0]) + sh2_ref[0]
    h2_ref[...] = h2
    qp = jnp.dot(h2.astype(BF16), wpq_ref[...], preferred_element_type=F32).astype(BF16)
    for hh in range(2 * HP):
        sct_ref[hh] = lax.dot_general(keys_ref[hh], qp[:, hh * DK_HALF:(hh + 1) * DK_HALF], _NT,
                                      preferred_element_type=F32)


def _merge(ya, yb, gs, x2d, wpa, wpb, wo, gt1, sc2, sh2, ln1_g, ln1_b, wpq, keys, tiles_per_batch):
    t, d = x2d.shape
    tn = TOKEN_TILE
    row = lambda i: (i, 0)
    brow = lambda i: (i // tiles_per_batch, 0, 0)
    return pl.pallas_call(
        _merge_kernel,
        out_shape=(jax.ShapeDtypeStruct((t, d), F32), jax.ShapeDtypeStruct((t, d), F32),
                   jax.ShapeDtypeStruct((2 * HP, N_KEYS, t), F32)),
        grid=(t // tn,),
        in_specs=[
            pl.BlockSpec((tn, d), row), pl.BlockSpec((tn, d), row), pl.BlockSpec((tn, 2 * d), row),
            pl.BlockSpec((tn, d), row),
            _resident(wpa.shape), _resident(wpb.shape), _resident(wo.shape),
            pl.BlockSpec((1, 1, d), brow), pl.BlockSpec((1, 1, d), brow), pl.BlockSpec((1, 1, d), brow),
            _full((1, d)), _full((1, d)), _resident(wpq.shape), _resident(keys.shape),
        ],
        out_specs=(pl.BlockSpec((tn, d), row), pl.BlockSpec((tn, d), row),
                   pl.BlockSpec((2 * HP, N_KEYS, tn), lambda i: (0, 0, i))),
        compiler_params=_cparams(("parallel",)),
        name="merge",
    )(ya, yb, gs, x2d, wpa, wpb, wo, gt1, sc2, sh2, ln1_g, ln1_b, wpq, keys)


SUBLANES = 8


def _tree(items, merge):
    while len(items) > 1:
        nxt = [merge(items[k], items[k + 1]) for k in range(0, len(items) - 1, 2)]
        if len(items) % 2:
            nxt.append(items[-1])
        items = nxt
    return items[0]


def _all_sublanes(x, op):
    for sh in (4, 2, 1):
        x = op(x, pltpu.roll(x, sh, axis=0))
    return x


def _top_keys(s_ref, k):
    nblk = s_ref.shape[0] // SUBLANES
    tl = s_ref.shape[1]
    sub = lax.broadcasted_iota(jnp.int32, (SUBLANES, tl), 0)
    rows = [sub + SUBLANES * b for b in range(nblk)]
    vs = [s_ref[SUBLANES * b:SUBLANES * (b + 1), :] for b in range(nblk)]

    def merge(a, b):
        c = a[0] >= b[0]
        return jnp.where(c, a[0], b[0]), jnp.where(c, a[1], b[1])

    vals, idxs = [], []
    for _ in range(k):
        v, i = _tree(list(zip(vs, rows)), merge)
        for sh in (4, 2, 1):
            pv, pi = pltpu.roll(v, sh, axis=0), pltpu.roll(i, sh, axis=0)
            c = (v > pv) | ((v == pv) & (i < pi))
            v, i = jnp.where(c, v, pv), jnp.where(c, i, pi)
        vals.append(v[0:1])
        idxs.append(i[0:1])
        vs = [jnp.where(rows[b] == i, -jnp.inf, vs[b]) for b in range(nblk)]
    return jnp.concatenate(vals, axis=0), jnp.concatenate(idxs, axis=0)


_CAND_BLOCKS = (("col", 0, 0), ("col", 0, 8), ("col", 1, 0),
                ("row", 0, 0, 2, 7), ("row", 0, 8, 8, 15), ("row", 1, 0, 2, 7),
                ("row", 2, 0, 2, 4), ("row", 3, 0, 2, 3), ("row", 4, 0, 2, 2))
assert TOPK == 16 and TOPK_HALF == 16 and sum(
    SUBLANES if blk[0] == "col" else blk[4] - blk[3] + 1 for blk in _CAND_BLOCKS) == sum(
    1 for a in range(16) for b in range(16) if (a + 1) * (b + 1) <= TOPK)


def _top_pairs(v1, i1, v2, i2, k):
    tl = v1.shape[1]
    sub = lax.broadcasted_iota(jnp.int32, (SUBLANES, tl), 0)
    big = TOPK_HALF * TOPK_HALF
    cs, flats, pays = [], [], []
    for blk in _CAND_BLOCKS:
        if blk[0] == "col":
            _, b, a0 = blk
            cs.append(v1[a0:a0 + SUBLANES] + v2[b:b + 1])
            pays.append((i1[a0:a0 + SUBLANES] * N_KEYS + i2[b:b + 1]) * SLAB_ROWS)
            flats.append((sub + a0) * TOPK_HALF + b)
        else:
            _, a, b0, lo, hi = blk
            ok = (sub + b0 >= lo) & (sub + b0 <= hi)
            cs.append(jnp.where(ok, v1[a:a + 1] + v2[b0:b0 + SUBLANES], -jnp.inf))
            pays.append((i1[a:a + 1] * N_KEYS + i2[b0:b0 + SUBLANES]) * SLAB_ROWS)
            flats.append(jnp.where(ok, a * TOPK_HALF + sub + b0, big))
    vals, outs = [], []
    for _ in range(k):
        m = _all_sublanes(_tree(cs, jnp.maximum), jnp.maximum)
        pos = _all_sublanes(_tree([jnp.where(c == m, f, big) for c, f in zip(cs, flats)], jnp.minimum), jnp.minimum)
        sels = [f == pos for f in flats]
        pay = _all_sublanes(_tree([jnp.where(s, p, -1) for s, p in zip(sels, pays)], jnp.maximum), jnp.maximum)
        vals.append(m[0:1])
        outs.append(pay[0:1])
        cs = [jnp.where(s, -jnp.inf, c) for s, c in zip(sels, cs)]
    return jnp.concatenate(vals, axis=0), jnp.concatenate(outs, axis=0)


def _route_kernel(sct_ref, idx_ref, g_ref, idxt_ref):
    def head(hh, carry):
        v1, i1 = _top_keys(sct_ref.at[2 * hh], TOPK_HALF)
        v2, i2 = _top_keys(sct_ref.at[2 * hh + 1], TOPK_HALF)
        gv, gi = _top_pairs(v1, i1, v2, i2, TOPK)
        e = jnp.exp(gv - gv[0:1])
        rows = pl.ds(pl.multiple_of(hh * TOPK, TOPK), TOPK)
        g_ref[rows, :] = e / jnp.sum(e, axis=0, keepdims=True)
        idxt_ref[rows, :] = gi.astype(F32)
        return carry

    lax.fori_loop(0, HP, head, 0)
    idx_ref[...] = jnp.transpose(idxt_ref[...]).astype(jnp.int32)


def _routing(sct):
    _, _, t = sct.shape
    tl = ROUTE_TILE
    nsel = HP * TOPK
    return pl.pallas_call(
        _route_kernel,
        out_shape=(jax.ShapeDtypeStruct((t, nsel), jnp.int32), jax.ShapeDtypeStruct((nsel, t), F32)),
        grid=(t // tl,),
        in_specs=[pl.BlockSpec((2 * HP, N_KEYS, tl), lambda i: (0, 0, i))],
        out_specs=(pl.BlockSpec((tl, nsel), lambda i: (i, 0)), pl.BlockSpec((nsel, tl), lambda i: (0, i))),
        scratch_shapes=[pltpu.VMEM((nsel, tl), F32)],
        compiler_params=_cparams(("parallel",)),
        name="routing",
    )(sct)


SLAB_ROWS = 4
_HI_MASK = 0xFFFF0000


def _unpack(slab):
    lo = lax.bitcast_convert_type(slab << 16, F32)
    hi = lax.bitcast_convert_type(slab & jnp.uint32(_HI_MASK), F32)
    return lo, hi


def _gather_slabs(idx_ref, tab_ref, slab_ref, t):
    group = 16
    for j0 in range(0, idx_ref.shape[1], group):
        row = idx_ref.at[t, pl.ds(j0, group)]
        for k in range(group):
            j = j0 + k
            first = pl.multiple_of(row[k], SLAB_ROWS)
            slab_ref[SLAB_ROWS * j:SLAB_ROWS * (j + 1), :] = tab_ref[pl.ds(first, SLAB_ROWS), :]


def _two_phase_loop(tb, gather, math, init):
    gather(0, 0)
    gather(1, 1)

    def two_tokens(i, carry):
        t0 = 2 * i
        carry = math(0, t0, carry)
        carry = math(1, t0 + 1, carry)
        gather(0, jnp.minimum(t0 + 2, tb - 1))
        gather(1, jnp.minimum(t0 + 3, tb - 1))
        return carry

    return lax.fori_loop(0, tb // 2, two_tokens, init)


def _slab_rows(slab_ref, s, nsel):
    return slab_ref[pl.ds(s, nsel, stride=4), :]


def _expert_down_kernel(idx_ref, x_ref, g_ref, u_ref, w_ref, slab0_ref, slab1_ref):
    tb = x_ref.shape[0]
    nsel = idx_ref.shape[1]
    slabs = (slab0_ref, slab1_ref)
    lane = lax.broadcasted_iota(jnp.int32, (nsel, tb), 1)

    def gather(buf, t):
        _gather_slabs(idx_ref, u_ref, slabs[buf], t)

    def math(buf, t, at):
        xv = x_ref[t]
        acc = None
        for s in range(4):
            lo, hi = _unpack(_slab_rows(slabs[buf], s, nsel))
            term = lo * xv[s:s + 1] + hi * xv[4 + s:5 + s]
            acc = term if acc is None else acc + term
        a = jnp.sum(acc, axis=1, keepdims=True)
        return jnp.where(lane == t, a, at)

    at = _two_phase_loop(tb, gather, math, jnp.zeros((nsel, tb), F32))
    w_ref[...] = jax.nn.gelu(at) * g_ref[...]


def _expert_down(idx, x3, g_t, u_tab):
    t = x3.shape[0]
    tb = EXPERT_TILE
    nsel = idx.shape[1]
    return pl.pallas_call(
        _expert_down_kernel,
        out_shape=jax.ShapeDtypeStruct((nsel, t), F32),
        grid=(t // tb,),
        in_specs=[
            pl.BlockSpec((tb, nsel), lambda i: (i, 0), memory_space=pltpu.SMEM),
            pl.BlockSpec((tb, 8, LANES), lambda i: (i, 0, 0)),
            pl.BlockSpec((nsel, tb), lambda i: (0, i)),
            _resident(u_tab.shape),
        ],
        out_specs=pl.BlockSpec((nsel, tb), lambda i: (0, i)),
        scratch_shapes=[pltpu.VMEM((4 * nsel, LANES), jnp.uint32), pltpu.VMEM((4 * nsel, LANES), jnp.uint32)],
        compiler_params=_cparams(("arbitrary",)),
        name="expert_down",
    )(idx, x3, g_t, u_tab)


def _expert_up_kernel(idx_ref, wt_ref, x1_ref, v_ref, gt2_ref, g2_ref, b2_ref, o_ref, acc_ref, slab0_ref, slab1_ref):
    tb = x1_ref.shape[0]
    nsel = idx_ref.shape[1]
    slabs = (slab0_ref, slab1_ref)

    def gather(buf, t):
        _gather_slabs(idx_ref, v_ref, slabs[buf], t)

    def math(buf, t, carry):
        wt = wt_ref[...]
        wb = jnp.take_along_axis(wt, jnp.full(wt.shape, t, jnp.int32), axis=1, mode="promise_in_bounds")
        los, his = [], []
        for s in range(4):
            lo, hi = _unpack(_slab_rows(slabs[buf], s, nsel))
            los.append(jnp.sum(lo * wb, axis=0, keepdims=True))
            his.append(jnp.sum(hi * wb, axis=0, keepdims=True))
        acc_ref[t] = jnp.concatenate(los + his, axis=0)
        return carry

    _two_phase_loop(tb, gather, math, 0)
    r = DEEPNORM_ALPHA * x1_ref[...] + gt2_ref[...] * acc_ref[...]
    n = r.shape[1] * r.shape[2]
    mu = jnp.sum(jnp.sum(r, axis=2, keepdims=True), axis=1, keepdims=True) / n
    c = r - mu
    var = jnp.sum(jnp.sum(c * c, axis=2, keepdims=True), axis=1, keepdims=True) / n
    o_ref[...] = c * lax.rsqrt(var + EPS) * g2_ref[...] + b2_ref[...]


def _expert_up(idx, w_t, x13, v_tab, gt2, ln2_g, ln2_b, tiles_per_batch):
    t = x13.shape[0]
    tb = EXPERT_TILE
    nsel = idx.shape[1]
    tok = pl.BlockSpec((tb, 8, LANES), lambda i: (i, 0, 0))
    return pl.pallas_call(
        _expert_up_kernel,
        out_shape=jax.ShapeDtypeStruct(x13.shape, F32),
        grid=(t // tb,),
        in_specs=[pl.BlockSpec((tb, nsel), lambda i: (i, 0), memory_space=pltpu.SMEM),
                  pl.BlockSpec((nsel, tb), lambda i: (0, i)), tok, _resident(v_tab.shape),
                  pl.BlockSpec((1, 8, LANES), lambda i: (i // tiles_per_batch, 0, 0)),
                  _full((1, 8, LANES)), _full((1, 8, LANES))],
        out_specs=tok,
        scratch_shapes=[pltpu.VMEM((tb, 8, LANES), F32),
                        pltpu.VMEM((4 * nsel, LANES), jnp.uint32), pltpu.VMEM((4 * nsel, LANES), jnp.uint32)],
        compiler_params=_cparams(("arbitrary",)),
        name="expert_up",
    )(idx, w_t, x13, v_tab, gt2, ln2_g, ln2_b)


def _pack_table(tab):
    e, d = tab.shape
    bits = lax.bitcast_convert_type(tab.astype(BF16), jnp.uint16).astype(jnp.uint32)
    half = d // 2
    word = bits[:, :half] | (bits[:, half:] << 16)
    assert half == SLAB_ROWS * LANES
    return word.reshape(e * SLAB_ROWS, LANES)


def _rope_tables(n, nctx):
    rows = n // GRID_W
    quarter = DA // 4
    inv = ROPE_BASE ** (-jnp.arange(quarter, dtype=F32) / quarter)
    row_ang = jnp.arange(rows, dtype=F32)[:, None] * inv
    col_ang = jnp.arange(GRID_W, dtype=F32)[:, None] * inv
    ang = jnp.concatenate([
        jnp.broadcast_to(row_ang[:, None, :], (rows, GRID_W, quarter)),
        jnp.broadcast_to(col_ang[None, :, :], (rows, GRID_W, quarter))], -1).reshape(n, 2 * quarter)
    cos = jnp.concatenate([jnp.ones((nctx, 2 * quarter), F32), jnp.cos(ang)], 0)
    sin = jnp.concatenate([jnp.zeros((nctx, 2 * quarter), F32), jnp.sin(ang)], 0)
    one, zero = jnp.ones_like(cos), jnp.zeros_like(sin)
    cf = jnp.concatenate([cos, cos, cos, cos], 1)
    sf = jnp.concatenate([-sin, sin, -sin, sin], 1)
    ch = jnp.concatenate([cos, cos, one, one], 1)
    sh = jnp.concatenate([-sin, sin, zero, zero], 1)
    return cf, sf, ch, sh


def kernel(x, c, ctx, c_ctx, w_mod, b_mod, w_in, w_q_up, q_norm_g, w_kv_up, kv_norm_g, lambda_q1, lambda_k1,
           lambda_q2, lambda_k2, subln_g, w_pa, w_pb, w_out, ln1_g, ln1_b, w_pq, peer_keys, peer_u, peer_v,
           ln2_g, ln2_b):
    bsz, n, d = x.shape
    nctx = ctx.shape[1]
    t = bsz * n
    assert d == 8 * LANES and n % TOKEN_TILE == 0 and nctx % TOKEN_TILE == 0 and n % GRID_W == 0
    assert t % ROUTE_TILE == 0 and n % EXPERT_TILE == 0 and w_mod.shape[0] == DEPTH

    rows = -(-(bsz + 1) // 8) * 8
    c_rows = jnp.concatenate([c, c_ctx[None, :], jnp.zeros((rows - bsz - 1, d), F32)], 0)
    mod = _modulation(c_rows, w_mod[0], b_mod[0][None, :])[:bsz + 1]
    sh1, sc1, gt1, sh2, sc2, gt2 = [m[:, None, :] for m in jnp.split(mod, 6, axis=-1)]

    w_all = jnp.concatenate([w_in[0], jnp.zeros((d, LANES - ROPE), F32)], 1).astype(BF16)
    wq = w_q_up[0].reshape(Q_LORA, HB, NOPE + ROPE)
    wq = jnp.concatenate([wq, jnp.zeros((Q_LORA, HB, 256 - NOPE - ROPE), F32)], -1).reshape(Q_LORA, HB * 256)
    cf, sf, ch, shh = _rope_tables(n, nctx)
    lam_rows = jnp.zeros((8, LANES), F32).at[0:4, 0:DA].set(
        jnp.stack([lambda_q1[0], lambda_k1[0], lambda_q2[0], lambda_k2[0]]))

    qa, qm, gs, ka, va, km, vb = _projection(
        x, ctx, sc1, sh1, w_all, q_norm_g[0][None, :], kv_norm_g[0][None, :], wq.astype(BF16),
        w_kv_up[0].astype(BF16), cf, sf, ch, shh)

    ya, yb = _mixers(qa, ka, va, qm, km, vb, lam_rows, subln_g[0][None, :])

    keys = peer_keys[0].reshape(2 * HP, N_KEYS, DK_HALF).astype(BF16)
    x1, h2, sct = _merge(
        ya.reshape(t, d), yb.reshape(t, d), gs.reshape(t, 2 * d), x.reshape(t, d),
        w_pa[0].astype(BF16), w_pb[0].astype(BF16), w_out[0].astype(BF16),
        gt1[:bsz], sc2[:bsz], sh2[:bsz], ln1_g[0][None, :], ln1_b[0][None, :],
        w_pq[0].astype(BF16), keys, n // TOKEN_TILE)

    idx, g_t = _routing(sct)

    w_t = _expert_down(idx, h2.reshape(t, 8, LANES), g_t, _pack_table(peer_u[0]))
    out = _expert_up(idx, w_t, x1.reshape(t, 8, LANES), _pack_table(peer_v[0]),
                     gt2[:bsz].reshape(bsz, 8, LANES), ln2_g[0].reshape(1, 8, LANES),
                     ln2_b[0].reshape(1, 8, LANES), n // EXPERT_TILE)
    return out.reshape(bsz, n, d)
```

```python
import functools
import math

import jax
import jax.numpy as jnp
from jax import lax
from jax.experimental import pallas as pl
from jax.experimental.pallas import tpu as pltpu

GRID_W = 64
HA, DA = 8, 64
HB, NOPE, ROPE, VB = 8, 128, 64, 128
Q_LORA, KV_LORA = 256, 256
N_KEYS = 128
HP, DK_HALF = 8, 128
TOPK_HALF, TOPK = 16, 16
ROPE_BASE = 10000.0
EPS = 1e-6
DEPTH = 1
DEEPNORM_ALPHA = (2 * DEPTH) ** 0.25
LAMBDA_INIT = 0.8 - 0.6 * math.exp(-0.3 * 0)
LOG2E = 1.0 / math.log(2.0)
QA_SCALE = DA ** -0.5 * LOG2E
QM_SCALE = (NOPE + ROPE) ** -0.5 * LOG2E

LANES = 128
VMEM_LIMIT_BYTES = 56 * 1024 * 1024

TOKEN_TILE = 256
Q_TILE = 512
ROUTE_TILE = 512
EXPERT_TILE = 128

F32 = jnp.float32
BF16 = jnp.bfloat16


def _cparams(sem):
    return pltpu.CompilerParams(dimension_semantics=sem, vmem_limit_bytes=VMEM_LIMIT_BYTES)


def _full(shape):
    n = len(shape)
    return pl.BlockSpec(shape, lambda *_: (0,) * n)


def _resident(shape):
    n = len(shape)
    return pl.BlockSpec(shape, lambda *_: (0,) * n, pipeline_mode=pl.Buffered(1))


def _mod_kernel(c_ref, w_ref, b_ref, o_ref):
    c = c_ref[...]
    s = c * jax.nn.sigmoid(c)
    o_ref[...] = jnp.dot(s, w_ref[...], preferred_element_type=F32) + b_ref[...]


def _modulation(c_rows, w_mod, b_mod):
    r, d = c_rows.shape
    n = w_mod.shape[1]
    tn = 1536
    return pl.pallas_call(
        _mod_kernel,
        out_shape=jax.ShapeDtypeStruct((r, n), F32),
        grid=(n // tn,),
        in_specs=[_full((r, d)), pl.BlockSpec((d, tn), lambda j: (0, j)), pl.BlockSpec((1, tn), lambda j: (0, j))],
        out_specs=pl.BlockSpec((r, tn), lambda j: (0, j)),
        compiler_params=_cparams(("arbitrary",)),
        name="modulation",
    )(c_rows, w_mod, b_mod)


def _swap_halves(blk):
    lane = lax.broadcasted_iota(jnp.int32, blk.shape, 1)
    up = pltpu.roll(blk, LANES - 32, axis=1)
    dn = pltpu.roll(blk, 32, axis=1)
    return jnp.where((lane % 64) < 32, up, dn)


def _rope_block(blk, c, s):
    return blk * c + _swap_halves(blk) * s


def _rms(x, g):
    return x * lax.rsqrt(jnp.mean(x * x, axis=-1, keepdims=True) + EPS) * g


def _proj_kernel(nct, x_ref, ctx_ref, sc_ref, sh_ref, w_ref, qg_ref, kvg_ref, wq_ref, wkv_ref,
                 cf_ref, sf_ref, ch_ref, shh_ref,
                 qa_ref, qm_ref, gs_ref, ka_ref, va_ref, km_ref, vb_ref):
    i = pl.program_id(1)
    d = x_ref.shape[-1]
    xt = jnp.where(i < nct, ctx_ref[0], x_ref[0])
    h = xt * (1.0 + sc_ref[0]) + sh_ref[0]
    hb = h.astype(BF16)
    cf, sf = cf_ref[...], sf_ref[...]
    ch, shh = ch_ref[...], shh_ref[...]

    def proj(lo, hi):
        return jnp.dot(hb, w_ref[:, lo:hi], preferred_element_type=F32)

    c_qa, c_ql, c_g = 0, HA * 2 * DA, HA * 2 * DA + Q_LORA
    c_ka = c_g + 2 * d
    c_va = c_ka + HA * 2 * DA
    c_kvl = c_va + HA * 2 * DA
    c_kr = c_kvl + KV_LORA

    ka = proj(c_ka, c_va)
    for b in range(ka.shape[1] // LANES):
        sl = slice(b * LANES, (b + 1) * LANES)
        ka_ref[0, :, sl] = _rope_block(ka[:, sl], cf, sf).astype(BF16)
    lane = lax.broadcasted_iota(jnp.int32, (hb.shape[0], LANES), 1)
    ones_col = jnp.where(lane == 0, 1.0, 0.0).astype(BF16)
    va = proj(c_va, c_kvl).astype(BF16)
    for hh in range(HA):
        va_ref[0, :, hh * 256:hh * 256 + 2 * DA] = va[:, hh * 2 * DA:(hh + 1) * 2 * DA]
        va_ref[0, :, hh * 256 + 2 * DA:(hh + 1) * 256] = ones_col
    kvl = _rms(proj(c_kvl, c_kr), kvg_ref[...]).astype(BF16)
    kv = jnp.dot(kvl, wkv_ref[...], preferred_element_type=F32)
    kr = _rope_block(proj(c_kr, c_kr + LANES), ch, shh).astype(BF16)
    for hh in range(HB):
        base = hh * (NOPE + VB)
        km_ref[0, :, hh * 256:hh * 256 + NOPE] = kv[:, base:base + NOPE].astype(BF16)
        km_ref[0, :, hh * 256 + NOPE:(hh + 1) * 256] = kr
        vb_ref[0, :, hh * 256:hh * 256 + VB] = kv[:, base + NOPE:base + NOPE + VB].astype(BF16)
        vb_ref[0, :, hh * 256 + VB:(hh + 1) * 256] = ones_col

    @pl.when(i >= nct)
    def _():
        qa = proj(c_qa, c_ql)
        for b in range(qa.shape[1] // LANES):
            sl = slice(b * LANES, (b + 1) * LANES)
            qa_ref[0, :, sl] = (_rope_block(qa[:, sl], cf, sf) * QA_SCALE).astype(BF16)
        ql = _rms(proj(c_ql, c_g), qg_ref[...]).astype(BF16)
        qb = jnp.dot(ql, wq_ref[...], preferred_element_type=F32) * QM_SCALE
        for hh in range(HB):
            qm_ref[0, :, hh * 256:hh * 256 + NOPE] = qb[:, hh * 256:hh * 256 + NOPE].astype(BF16)
            qm_ref[0, :, hh * 256 + NOPE:(hh + 1) * 256] = _rope_block(
                qb[:, hh * 256 + NOPE:(hh + 1) * 256], ch, shh).astype(BF16)
        gs_ref[0] = jax.nn.sigmoid(proj(c_g, c_ka)).astype(BF16)


def _projection(x, ctx, sc1, sh1, w_all, q_norm_g, kv_norm_g, wq, wkv, cf, sf, ch, shh):
    bsz, n, d = x.shape
    nctx = ctx.shape[1]
    tn = TOKEN_TILE
    nct = nctx // tn
    s = nctx + n
    ncols = w_all.shape[1]

    def lat(b, i):
        return (b, jnp.maximum(i - nct, 0), 0)

    def cx(b, i):
        return (b, jnp.minimum(i, nct - 1), 0)

    def modrow(b, i):
        return (jnp.where(i < nct, bsz, b), 0, 0)

    def allk(b, i):
        return (b, i, 0)

    def tab(b, i):
        return (i, 0)

    bf = lambda w: jax.ShapeDtypeStruct((bsz, n, w), BF16)
    bfs = lambda w: jax.ShapeDtypeStruct((bsz, s, w), BF16)
    return pl.pallas_call(
        functools.partial(_proj_kernel, nct),
        out_shape=(bf(HA * 2 * DA), bf(HB * 256), bf(2 * d), bfs(HA * 2 * DA), bfs(HA * 256), bfs(HB * 256), bfs(HB * 256)),
        grid=(bsz, s // tn),
        in_specs=[
            pl.BlockSpec((1, tn, d), lat), pl.BlockSpec((1, tn, d), cx),
            pl.BlockSpec((1, 1, d), modrow), pl.BlockSpec((1, 1, d), modrow),
            _resident((d, ncols)), _full((1, Q_LORA)), _full((1, KV_LORA)),
            _resident(wq.shape), _resident(wkv.shape),
            pl.BlockSpec((tn, LANES), tab), pl.BlockSpec((tn, LANES), tab),
            pl.BlockSpec((tn, LANES), tab), pl.BlockSpec((tn, LANES), tab),
        ],
        out_specs=(
            pl.BlockSpec((1, tn, HA * 2 * DA), lat), pl.BlockSpec((1, tn, HB * 256), lat),
            pl.BlockSpec((1, tn, 2 * d), lat),
            pl.BlockSpec((1, tn, HA * 2 * DA), allk), pl.BlockSpec((1, tn, HA * 256), allk),
            pl.BlockSpec((1, tn, HB * 256), allk), pl.BlockSpec((1, tn, HB * 256), allk),
        ),
        compiler_params=_cparams(("parallel", "arbitrary")),
        name="projection",
    )(x, ctx, sc1, sh1, w_all, q_norm_g, kv_norm_g, wq, wkv, cf, sf, ch, shh)


_NT = (((1,), (1,)), ((), ()))


def _softmax_pv(q, k, v_ones):
    s = lax.dot_general(q, k, _NT, preferred_element_type=F32)
    m = jnp.max(s, axis=-1, keepdims=True)
    p = jnp.exp2(s - m).astype(BF16)
    o = jnp.dot(p, v_ones, preferred_element_type=F32)
    w = o.shape[1] // 2
    return o[:, :w] / o[:, w:w + 1]


def _mixers_kernel(qa_ref, ka_ref, va_ref, qm_ref, km_ref, vb_ref, lam_ref, g_ref, ya_ref, yb_ref):
    q = qa_ref[0]
    k = ka_ref[0]
    v = va_ref[0]
    lane = lax.broadcasted_iota(jnp.int32, q.shape, 1)
    zero = jnp.zeros_like(q)
    o1 = _softmax_pv(jnp.where(lane < DA, q, zero), k, v)
    o2 = _softmax_pv(jnp.where(lane < DA, zero, q), k, v)
    lv = lam_ref[...]
    lam = (jnp.exp(jnp.sum(lv[0:1] * lv[1:2], axis=-1, keepdims=True))
           - jnp.exp(jnp.sum(lv[2:3] * lv[3:4], axis=-1, keepdims=True)) + LAMBDA_INIT)
    o = o1 - lam * o2
    y = o * lax.rsqrt(jnp.mean(o * o, axis=-1, keepdims=True) + EPS) * g_ref[...]
    ya_ref[0] = (y * (1.0 - LAMBDA_INIT)).astype(BF16)
    yb_ref[0] = _softmax_pv(qm_ref[0], km_ref[0], vb_ref[0]).astype(BF16)


def _mixers(qa, ka, va, qm, km, vb, lam_rows, subln_g):
    assert HA == HB
    bsz, n, w = qa.shape
    s = ka.shape[1]
    tq = Q_TILE
    qspec = lambda width: pl.BlockSpec((1, tq, width), lambda b, h, i: (b, i, h))
    kspec = lambda width: pl.BlockSpec((1, s, width), lambda b, h, i: (b, 0, h))
    return pl.pallas_call(
        _mixers_kernel,
        out_shape=(jax.ShapeDtypeStruct((bsz, n, w), BF16), jax.ShapeDtypeStruct((bsz, n, HB * VB), BF16)),
        grid=(bsz, HA, n // tq),
        in_specs=[qspec(LANES), kspec(LANES), kspec(256), qspec(256), kspec(256), kspec(256),
                  _full(lam_rows.shape), _full(subln_g.shape)],
        out_specs=(qspec(LANES), qspec(VB)),
        compiler_params=_cparams(("parallel", "parallel", "arbitrary")),
        name="token_mixers",
    )(qa, ka, va, qm, km, vb, lam_rows, subln_g)


def _layer_norm(r, g, b):
    mu = jnp.mean(r, axis=-1, keepdims=True)
    c = r - mu
    var = jnp.mean(c * c, axis=-1, keepdims=True)
    return c * lax.rsqrt(var + EPS) * g + b


def _merge_kernel(ya_ref, yb_ref, gs_ref, x_ref, wpa_ref, wpb_ref, wo_ref, gt1_ref, sc2_ref, sh2_ref,
                  g1_ref, b1_ref, wpq_ref, keys_ref, x1_ref, h2_ref, sct_ref):
    d = x_ref.shape[-1]
    gs = gs_ref[...].astype(F32)
    za = jnp.dot(ya_ref[...], wpa_ref[...], preferred_element_type=F32)
    zb = jnp.dot(yb_ref[...], wpb_ref[...], preferred_element_type=F32)
    z = gs[:, :d] * za + gs[:, d:] * zb
    y = jnp.dot(z.astype(BF16), wo_ref[...], preferred_element_type=F32)
    x1 = _layer_norm(DEEPNORM_ALPHA * x_ref[...] + gt1_ref[0] * y, g1_ref[...], b1_ref[...])
    h2 = x1 * (1.0 + sc2_ref[0]) + sh2_ref[0]
    tn = x1.shape[0]
    for ch in range(d // LANES):
        x1_ref[pl.ds(ch, tn, stride=8), :] = x1[:, ch * LANES:(ch + 1) * LANES]
        h2_ref[pl.ds(ch, tn, stride=8), :] = h2[:, ch * LANES:(ch + 1) * LANES]
    qp = jnp.dot(h2.astype(BF16), wpq_ref[...], preferred_element_type=F32).astype(BF16)
    for hh in range(2 * HP):
        sct_ref[hh] = lax.dot_general(keys_ref[hh], qp[:, hh * DK_HALF:(hh + 1) * DK_HALF], _NT,
                                      preferred_element_type=F32)


def _merge(ya, yb, gs, x2d, wpa, wpb, wo, gt1, sc2, sh2, ln1_g, ln1_b, wpq, keys, tiles_per_batch):
    t, d = x2d.shape
    tn = TOKEN_TILE
    row = lambda i: (i, 0)
    brow = lambda i: (i // tiles_per_batch, 0, 0)
    return pl.pallas_call(
        _merge_kernel,
        out_shape=(jax.ShapeDtypeStruct((t * 8, LANES), F32), jax.ShapeDtypeStruct((t * 8, LANES), F32),
                   jax.ShapeDtypeStruct((2 * HP, N_KEYS, t), F32)),
        grid=(t // tn,),
        in_specs=[
            pl.BlockSpec((tn, d), row), pl.BlockSpec((tn, d), row), pl.BlockSpec((tn, 2 * d), row),
            pl.BlockSpec((tn, d), row),
            _resident(wpa.shape), _resident(wpb.shape), _resident(wo.shape),
            pl.BlockSpec((1, 1, d), brow), pl.BlockSpec((1, 1, d), brow), pl.BlockSpec((1, 1, d), brow),
            _full((1, d)), _full((1, d)), _resident(wpq.shape), _resident(keys.shape),
        ],
        out_specs=(pl.BlockSpec((tn * 8, LANES), row), pl.BlockSpec((tn * 8, LANES), row),
                   pl.BlockSpec((2 * HP, N_KEYS, tn), lambda i: (0, 0, i))),
        compiler_params=_cparams(("parallel",)),
        name="merge",
    )(ya, yb, gs, x2d, wpa, wpb, wo, gt1, sc2, sh2, ln1_g, ln1_b, wpq, keys)


SUBLANES = 8


def _tree(items, merge):
    while len(items) > 1:
        nxt = [merge(items[k], items[k + 1]) for k in range(0, len(items) - 1, 2)]
        if len(items) % 2:
            nxt.append(items[-1])
        items = nxt
    return items[0]


def _all_sublanes(x, op):
    for sh in (4, 2, 1):
        x = op(x, pltpu.roll(x, sh, axis=0))
    return x


def _top_keys(s_ref, k):
    nblk = s_ref.shape[0] // SUBLANES
    tl = s_ref.shape[1]
    sub = lax.broadcasted_iota(jnp.int32, (SUBLANES, tl), 0)
    rows = [sub + SUBLANES * b for b in range(nblk)]
    vs = [s_ref[SUBLANES * b:SUBLANES * (b + 1), :] for b in range(nblk)]

    def merge(a, b):
        c = a[0] >= b[0]
        return jnp.where(c, a[0], b[0]), jnp.where(c, a[1], b[1])

    vals, idxs = [], []
    for _ in range(k):
        v, i = _tree(list(zip(vs, rows)), merge)
        for sh in (4, 2, 1):
            pv, pi = pltpu.roll(v, sh, axis=0), pltpu.roll(i, sh, axis=0)
            c = (v > pv) | ((v == pv) & (i < pi))
            v, i = jnp.where(c, v, pv), jnp.where(c, i, pi)
        vals.append(v[0:1])
        idxs.append(i[0:1])
        vs = [jnp.where(rows[b] == i, -jnp.inf, vs[b]) for b in range(nblk)]
    return jnp.concatenate(vals, axis=0), jnp.concatenate(idxs, axis=0)


_CAND_BLOCKS = (("col", 0, 0), ("col", 0, 8), ("col", 1, 0),
                ("row", 0, 0, 2, 7), ("row", 0, 8, 8, 15), ("row", 1, 0, 2, 7),
                ("row", 2, 0, 2, 4), ("row", 3, 0, 2, 3), ("row", 4, 0, 2, 2))
assert TOPK == 16 and TOPK_HALF == 16 and sum(
    SUBLANES if blk[0] == "col" else blk[4] - blk[3] + 1 for blk in _CAND_BLOCKS) == sum(
    1 for a in range(16) for b in range(16) if (a + 1) * (b + 1) <= TOPK)


def _top_pairs(v1, i1, v2, i2, k):
    tl = v1.shape[1]
    sub = lax.broadcasted_iota(jnp.int32, (SUBLANES, tl), 0)
    big = TOPK_HALF * TOPK_HALF
    cs, flats, pays = [], [], []
    for blk in _CAND_BLOCKS:
        if blk[0] == "col":
            _, b, a0 = blk
            cs.append(v1[a0:a0 + SUBLANES] + v2[b:b + 1])
            pays.append((i1[a0:a0 + SUBLANES] * N_KEYS + i2[b:b + 1]) * SLAB_ROWS)
            flats.append((sub + a0) * TOPK_HALF + b)
        else:
            _, a, b0, lo, hi = blk
            ok = (sub + b0 >= lo) & (sub + b0 <= hi)
            cs.append(jnp.where(ok, v1[a:a + 1] + v2[b0:b0 + SUBLANES], -jnp.inf))
            pays.append((i1[a:a + 1] * N_KEYS + i2[b0:b0 + SUBLANES]) * SLAB_ROWS)
            flats.append(jnp.where(ok, a * TOPK_HALF + sub + b0, big))
    vals, outs = [], []
    for _ in range(k):
        m = _all_sublanes(_tree(cs, jnp.maximum), jnp.maximum)
        pos = _all_sublanes(_tree([jnp.where(c == m, f, big) for c, f in zip(cs, flats)], jnp.minimum), jnp.minimum)
        sels = [f == pos for f in flats]
        pay = _all_sublanes(_tree([jnp.where(s, p, -1) for s, p in zip(sels, pays)], jnp.maximum), jnp.maximum)
        vals.append(m[0:1])
        outs.append(pay[0:1])
        cs = [jnp.where(s, -jnp.inf, c) for s, c in zip(sels, cs)]
    return jnp.concatenate(vals, axis=0), jnp.concatenate(outs, axis=0)


def _route_kernel(sct_ref, idx_ref, g_ref, idxt_ref):
    def head(hh, carry):
        v1, i1 = _top_keys(sct_ref.at[2 * hh], TOPK_HALF)
        v2, i2 = _top_keys(sct_ref.at[2 * hh + 1], TOPK_HALF)
        gv, gi = _top_pairs(v1, i1, v2, i2, TOPK)
        e = jnp.exp(gv - gv[0:1])
        rows = pl.ds(pl.multiple_of(hh * TOPK, TOPK), TOPK)
        g_ref[rows, :] = e / jnp.sum(e, axis=0, keepdims=True)
        idxt_ref[rows, :] = gi.astype(F32)
        return carry

    lax.fori_loop(0, HP, head, 0)
    idx_ref[...] = jnp.transpose(idxt_ref[...]).astype(jnp.int32)


def _routing(sct):
    _, _, t = sct.shape
    tl = ROUTE_TILE
    nsel = HP * TOPK
    return pl.pallas_call(
        _route_kernel,
        out_shape=(jax.ShapeDtypeStruct((t, nsel), jnp.int32), jax.ShapeDtypeStruct((nsel, t), F32)),
        grid=(t // tl,),
        in_specs=[pl.BlockSpec((2 * HP, N_KEYS, tl), lambda i: (0, 0, i))],
        out_specs=(pl.BlockSpec((tl, nsel), lambda i: (i, 0)), pl.BlockSpec((nsel, tl), lambda i: (0, i))),
        scratch_shapes=[pltpu.VMEM((nsel, tl), F32)],
        compiler_params=_cparams(("parallel",)),
        name="routing",
    )(sct)


SLAB_ROWS = 4
_HI_MASK = 0xFFFF0000


def _unpack(slab):
    lo = lax.bitcast_convert_type(slab << 16, F32)
    hi = lax.bitcast_convert_type(slab & jnp.uint32(_HI_MASK), F32)
    return lo, hi


def _gather_slabs(idx_ref, tab_ref, slab_ref, t):
    group = 16
    for j0 in range(0, idx_ref.shape[1], group):
        row = idx_ref.at[t, pl.ds(j0, group)]
        for k in range(group):
            j = j0 + k
            first = pl.multiple_of(row[k], SLAB_ROWS)
            slab_ref[SLAB_ROWS * j:SLAB_ROWS * (j + 1), :] = tab_ref[pl.ds(first, SLAB_ROWS), :]


def _two_phase_loop(tb, gather, math, init):
    gather(0, 0)
    gather(1, 1)

    def two_tokens(i, carry):
        t0 = 2 * i
        carry = math(0, t0, carry)
        carry = math(1, t0 + 1, carry)
        gather(0, jnp.minimum(t0 + 2, tb - 1))
        gather(1, jnp.minimum(t0 + 3, tb - 1))
        return carry

    return lax.fori_loop(0, tb // 2, two_tokens, init)


def _slab_rows(slab_ref, s, nsel):
    return slab_ref[pl.ds(s, nsel, stride=4), :]


def _expert_down_kernel(idx_ref, x_ref, g_ref, u_ref, w_ref, slab0_ref, slab1_ref):
    tb = x_ref.shape[0]
    nsel = idx_ref.shape[1]
    slabs = (slab0_ref, slab1_ref)
    lane = lax.broadcasted_iota(jnp.int32, (nsel, tb), 1)

    def gather(buf, t):
        _gather_slabs(idx_ref, u_ref, slabs[buf], t)

    def math(buf, t, at):
        xv = x_ref[t]
        acc = None
        for s in range(SLAB_ROWS):
            lo, hi = _unpack(_slab_rows(slabs[buf], s, nsel))
            term = lo * xv[s:s + 1] + hi * xv[SLAB_ROWS + s:SLAB_ROWS + s + 1]
            acc = term if acc is None else acc + term
        a = jnp.sum(acc, axis=1, keepdims=True)
        return jnp.where(lane == t, a, at)

    at = _two_phase_loop(tb, gather, math, jnp.zeros((nsel, tb), F32))
    w_ref[...] = jax.nn.gelu(at) * g_ref[...]


def _expert_down(idx, x3, g_t, u_tab):
    t = x3.shape[0]
    tb = EXPERT_TILE
    nsel = idx.shape[1]
    return pl.pallas_call(
        _expert_down_kernel,
        out_shape=jax.ShapeDtypeStruct((nsel, t), F32),
        grid=(t // tb,),
        in_specs=[
            pl.BlockSpec((tb, nsel), lambda i: (i, 0), memory_space=pltpu.SMEM),
            pl.BlockSpec((tb, 8, LANES), lambda i: (i, 0, 0)),
            pl.BlockSpec((nsel, tb), lambda i: (0, i)),
            _resident(u_tab.shape),
        ],
        out_specs=pl.BlockSpec((nsel, tb), lambda i: (0, i)),
        scratch_shapes=[pltpu.VMEM((4 * nsel, LANES), jnp.uint32), pltpu.VMEM((4 * nsel, LANES), jnp.uint32)],
        compiler_params=_cparams(("arbitrary",)),
        name="expert_down",
    )(idx, x3, g_t, u_tab)


def _expert_up_kernel(idx_ref, wt_ref, x1_ref, v_ref, gt2_ref, g2_ref, b2_ref, o_ref, acc_ref, y_ref,
                      slab0_ref, slab1_ref):
    tb = x1_ref.shape[0]
    nsel = idx_ref.shape[1]
    slabs = (slab0_ref, slab1_ref)

    def gather(buf, t):
        _gather_slabs(idx_ref, v_ref, slabs[buf], t)

    def math(buf, t, carry):
        wt = wt_ref[...]
        wb = jnp.take_along_axis(wt, jnp.full(wt.shape, t, jnp.int32), axis=1, mode="promise_in_bounds")
        los, his = [], []
        for s in range(SLAB_ROWS):
            lo, hi = _unpack(_slab_rows(slabs[buf], s, nsel))
            los.append(jnp.sum(lo * wb, axis=0, keepdims=True))
            his.append(jnp.sum(hi * wb, axis=0, keepdims=True))
        acc_ref[t] = jnp.concatenate(los + his, axis=0)
        return carry

    _two_phase_loop(tb, gather, math, 0)
    r = DEEPNORM_ALPHA * x1_ref[...] + gt2_ref[...] * acc_ref[...]
    n = r.shape[1] * r.shape[2]
    mu = jnp.sum(jnp.sum(r, axis=2, keepdims=True), axis=1, keepdims=True) / n
    c = r - mu
    var = jnp.sum(jnp.sum(c * c, axis=2, keepdims=True), axis=1, keepdims=True) / n
    y_ref[...] = (c * lax.rsqrt(var + EPS) * g2_ref[...] + b2_ref[...]).reshape(tb * 8, LANES)
    for ch in range(8):
        o_ref[:, ch * LANES:(ch + 1) * LANES] = y_ref[pl.ds(ch, tb, stride=8), :]


def _expert_up(idx, w_t, x13, v_tab, gt2, ln2_g, ln2_b, tiles_per_batch):
    t = x13.shape[0]
    tb = EXPERT_TILE
    nsel = idx.shape[1]
    tok = pl.BlockSpec((tb, 8, LANES), lambda i: (i, 0, 0))
    return pl.pallas_call(
        _expert_up_kernel,
        out_shape=jax.ShapeDtypeStruct((t, 8 * LANES), F32),
        grid=(t // tb,),
        in_specs=[pl.BlockSpec((tb, nsel), lambda i: (i, 0), memory_space=pltpu.SMEM),
                  pl.BlockSpec((nsel, tb), lambda i: (0, i)), tok, _resident(v_tab.shape),
                  pl.BlockSpec((1, 8, LANES), lambda i: (i // tiles_per_batch, 0, 0)),
                  _full((1, 8, LANES)), _full((1, 8, LANES))],
        out_specs=pl.BlockSpec((tb, 8 * LANES), lambda i: (i, 0)),
        scratch_shapes=[pltpu.VMEM((tb, 8, LANES), F32), pltpu.VMEM((tb * 8, LANES), F32),
                        pltpu.VMEM((4 * nsel, LANES), jnp.uint32), pltpu.VMEM((4 * nsel, LANES), jnp.uint32)],
        compiler_params=_cparams(("arbitrary",)),
        name="expert_up",
    )(idx, w_t, x13, v_tab, gt2, ln2_g, ln2_b)


def _pack_table(tab):
    e, d = tab.shape
    bits = lax.bitcast_convert_type(tab.astype(BF16), jnp.uint16).astype(jnp.uint32)
    half = d // 2
    word = bits[:, :half] | (bits[:, half:] << 16)
    assert half == SLAB_ROWS * LANES
    return word.reshape(e * SLAB_ROWS, LANES)


def _rope_tables(n, nctx):
    rows = n // GRID_W
    quarter = DA // 4
    inv = ROPE_BASE ** (-jnp.arange(quarter, dtype=F32) / quarter)
    row_ang = jnp.arange(rows, dtype=F32)[:, None] * inv
    col_ang = jnp.arange(GRID_W, dtype=F32)[:, None] * inv
    ang = jnp.concatenate([
        jnp.broadcast_to(row_ang[:, None, :], (rows, GRID_W, quarter)),
        jnp.broadcast_to(col_ang[None, :, :], (rows, GRID_W, quarter))], -1).reshape(n, 2 * quarter)
    cos = jnp.concatenate([jnp.ones((nctx, 2 * quarter), F32), jnp.cos(ang)], 0)
    sin = jnp.concatenate([jnp.zeros((nctx, 2 * quarter), F32), jnp.sin(ang)], 0)
    one, zero = jnp.ones_like(cos), jnp.zeros_like(sin)
    cf = jnp.concatenate([cos, cos, cos, cos], 1)
    sf = jnp.concatenate([-sin, sin, -sin, sin], 1)
    ch = jnp.concatenate([cos, cos, one, one], 1)
    sh = jnp.concatenate([-sin, sin, zero, zero], 1)
    return cf, sf, ch, sh


def kernel(x, c, ctx, c_ctx, w_mod, b_mod, w_in, w_q_up, q_norm_g, w_kv_up, kv_norm_g, lambda_q1, lambda_k1,
           lambda_q2, lambda_k2, subln_g, w_pa, w_pb, w_out, ln1_g, ln1_b, w_pq, peer_keys, peer_u, peer_v,
           ln2_g, ln2_b):
    bsz, n, d = x.shape
    nctx = ctx.shape[1]
    t = bsz * n
    assert d == 8 * LANES and n % TOKEN_TILE == 0 and nctx % TOKEN_TILE == 0 and n % GRID_W == 0
    assert t % ROUTE_TILE == 0 and n % EXPERT_TILE == 0 and w_mod.shape[0] == DEPTH

    rows = -(-(bsz + 1) // 8) * 8
    c_rows = jnp.concatenate([c, c_ctx[None, :], jnp.zeros((rows - bsz - 1, d), F32)], 0)
    mod = _modulation(c_rows, w_mod[0], b_mod[0][None, :])[:bsz + 1]
    sh1, sc1, gt1, sh2, sc2, gt2 = [m[:, None, :] for m in jnp.split(mod, 6, axis=-1)]

    w_all = jnp.concatenate([w_in[0], jnp.zeros((d, LANES - ROPE), F32)], 1).astype(BF16)
    wq = w_q_up[0].reshape(Q_LORA, HB, NOPE + ROPE)
    wq = jnp.concatenate([wq, jnp.zeros((Q_LORA, HB, 256 - NOPE - ROPE), F32)], -1).reshape(Q_LORA, HB * 256)
    cf, sf, ch, shh = _rope_tables(n, nctx)
    lam_rows = jnp.zeros((8, LANES), F32).at[0:4, 0:DA].set(
        jnp.stack([lambda_q1[0], lambda_k1[0], lambda_q2[0], lambda_k2[0]]))

    qa, qm, gs, ka, va, km, vb = _projection(
        x, ctx, sc1, sh1, w_all, q_norm_g[0][None, :], kv_norm_g[0][None, :], wq.astype(BF16),
        w_kv_up[0].astype(BF16), cf, sf, ch, shh)

    ya, yb = _mixers(qa, ka, va, qm, km, vb, lam_rows, subln_g[0][None, :])

    keys = peer_keys[0].reshape(2 * HP, N_KEYS, DK_HALF).astype(BF16)
    x1, h2, sct = _merge(
        ya.reshape(t, d), yb.reshape(t, d), gs.reshape(t, 2 * d), x.reshape(t, d),
        w_pa[0].astype(BF16), w_pb[0].astype(BF16), w_out[0].astype(BF16),
        gt1[:bsz], sc2[:bsz], sh2[:bsz], ln1_g[0][None, :], ln1_b[0][None, :],
        w_pq[0].astype(BF16), keys, n // TOKEN_TILE)

    idx, g_t = _routing(sct)

    w_t = _expert_down(idx, h2.reshape(t, 8, LANES), g_t, _pack_table(peer_u[0]))
    out = _expert_up(idx, w_t, x1.reshape(t, 8, LANES), _pack_table(peer_v[0]),
                     gt2[:bsz].reshape(bsz, 8, LANES), ln2_g[0].reshape(1, 8, LANES),
                     ln2_b[0].reshape(1, 8, LANES), n // EXPERT_TILE)
    return out.reshape(bsz, n, d)
```

```python
import functools
import math

import jax
import jax.numpy as jnp
from jax import lax
from jax.experimental import pallas as pl
from jax.experimental.pallas import tpu as pltpu

GRID_W = 64
HA, DA = 8, 64
HB, NOPE, ROPE, VB = 8, 128, 64, 128
Q_LORA, KV_LORA = 256, 256
N_KEYS = 128
HP, DK_HALF = 8, 128
TOPK_HALF, TOPK = 16, 16
ROPE_BASE = 10000.0
EPS = 1e-6
DEPTH = 1
DEEPNORM_ALPHA = (2 * DEPTH) ** 0.25
LAMBDA_INIT = 0.8 - 0.6 * math.exp(-0.3 * 0)
LOG2E = 1.0 / math.log(2.0)
QA_SCALE = DA ** -0.5 * LOG2E
QM_SCALE = (NOPE + ROPE) ** -0.5 * LOG2E

LANES = 128
VMEM_LIMIT_BYTES = 56 * 1024 * 1024

TOKEN_TILE = 256
Q_TILE = 1024
ROUTE_TILE = 512
EXPERT_TILE = 128

F32 = jnp.float32
BF16 = jnp.bfloat16


def _cparams(sem):
    return pltpu.CompilerParams(dimension_semantics=sem, vmem_limit_bytes=VMEM_LIMIT_BYTES)


def _full(shape):
    n = len(shape)
    return pl.BlockSpec(shape, lambda *_: (0,) * n)


def _resident(shape):
    n = len(shape)
    return pl.BlockSpec(shape, lambda *_: (0,) * n, pipeline_mode=pl.Buffered(1))


def _mod_kernel(c_ref, w_ref, b_ref, o_ref):
    c = c_ref[...]
    s = c * jax.nn.sigmoid(c)
    o_ref[...] = jnp.dot(s, w_ref[...], preferred_element_type=F32) + b_ref[...]


def _modulation(c_rows, w_mod, b_mod):
    r, d = c_rows.shape
    n = w_mod.shape[1]
    tn = 1536
    return pl.pallas_call(
        _mod_kernel,
        out_shape=jax.ShapeDtypeStruct((r, n), F32),
        grid=(n // tn,),
        in_specs=[_full((r, d)), pl.BlockSpec((d, tn), lambda j: (0, j)), pl.BlockSpec((1, tn), lambda j: (0, j))],
        out_specs=pl.BlockSpec((r, tn), lambda j: (0, j)),
        compiler_params=_cparams(("arbitrary",)),
        name="modulation",
    )(c_rows, w_mod, b_mod)


def _swap_halves(blk):
    lane = lax.broadcasted_iota(jnp.int32, blk.shape, 1)
    up = pltpu.roll(blk, LANES - 32, axis=1)
    dn = pltpu.roll(blk, 32, axis=1)
    return jnp.where((lane % 64) < 32, up, dn)


def _rope_block(blk, c, s):
    return blk * c + _swap_halves(blk) * s


def _rms(x, g):
    return x * lax.rsqrt(jnp.mean(x * x, axis=-1, keepdims=True) + EPS) * g


def _proj_kernel(nct, x_ref, ctx_ref, sc_ref, sh_ref, w_ref, qg_ref, kvg_ref, wq_ref, wkv_ref,
                 cf_ref, sf_ref, ch_ref, shh_ref,
                 qa_ref, qm_ref, gs_ref, ka_ref, va_ref, km_ref, vb_ref):
    i = pl.program_id(1)
    d = x_ref.shape[-1]
    xt = jnp.where(i < nct, ctx_ref[0], x_ref[0])
    h = xt * (1.0 + sc_ref[0]) + sh_ref[0]
    hb = h.astype(BF16)
    cf, sf = cf_ref[...], sf_ref[...]
    ch, shh = ch_ref[...], shh_ref[...]

    def proj(lo, hi):
        return jnp.dot(hb, w_ref[:, lo:hi], preferred_element_type=F32)

    c_qa, c_ql, c_g = 0, HA * 2 * DA, HA * 2 * DA + Q_LORA
    c_ka = c_g + 2 * d
    c_va = c_ka + HA * 2 * DA
    c_kvl = c_va + HA * 2 * DA
    c_kr = c_kvl + KV_LORA

    ka = proj(c_ka, c_va)
    for b in range(ka.shape[1] // LANES):
        sl = slice(b * LANES, (b + 1) * LANES)
        ka_ref[0, :, sl] = _rope_block(ka[:, sl], cf, sf).astype(BF16)
    lane = lax.broadcasted_iota(jnp.int32, (hb.shape[0], LANES), 1)
    ones_col = jnp.where(lane == 0, 1.0, 0.0).astype(BF16)
    va = proj(c_va, c_kvl).astype(BF16)
    for hh in range(HA):
        va_ref[0, :, hh * 256:hh * 256 + 2 * DA] = va[:, hh * 2 * DA:(hh + 1) * 2 * DA]
        va_ref[0, :, hh * 256 + 2 * DA:(hh + 1) * 256] = ones_col
    kvl = _rms(proj(c_kvl, c_kr), kvg_ref[...]).astype(BF16)
    kv = jnp.dot(kvl, wkv_ref[...], preferred_element_type=F32)
    kr = _rope_block(proj(c_kr, c_kr + LANES), ch, shh).astype(BF16)
    for hh in range(HB):
        base = hh * (NOPE + VB)
        km_ref[0, :, hh * 256:hh * 256 + NOPE] = kv[:, base:base + NOPE].astype(BF16)
        km_ref[0, :, hh * 256 + NOPE:(hh + 1) * 256] = kr
        vb_ref[0, :, hh * 256:hh * 256 + VB] = kv[:, base + NOPE:base + NOPE + VB].astype(BF16)
        vb_ref[0, :, hh * 256 + VB:(hh + 1) * 256] = ones_col

    @pl.when(i >= nct)
    def _():
        qa = proj(c_qa, c_ql)
        for b in range(qa.shape[1] // LANES):
            sl = slice(b * LANES, (b + 1) * LANES)
            qa_ref[0, :, sl] = (_rope_block(qa[:, sl], cf, sf) * QA_SCALE).astype(BF16)
        ql = _rms(proj(c_ql, c_g), qg_ref[...]).astype(BF16)
        qb = jnp.dot(ql, wq_ref[...], preferred_element_type=F32) * QM_SCALE
        for hh in range(HB):
            qm_ref[0, :, hh * 256:hh * 256 + NOPE] = qb[:, hh * 256:hh * 256 + NOPE].astype(BF16)
            qm_ref[0, :, hh * 256 + NOPE:(hh + 1) * 256] = _rope_block(
                qb[:, hh * 256 + NOPE:(hh + 1) * 256], ch, shh).astype(BF16)
        gs_ref[0] = jax.nn.sigmoid(proj(c_g, c_ka)).astype(BF16)


def _projection(x, ctx, sc1, sh1, w_all, q_norm_g, kv_norm_g, wq, wkv, cf, sf, ch, shh):
    bsz, n, d = x.shape
    nctx = ctx.shape[1]
    tn = TOKEN_TILE
    nct = nctx // tn
    s = nctx + n
    ncols = w_all.shape[1]

    def lat(b, i):
        return (b, jnp.maximum(i - nct, 0), 0)

    def cx(b, i):
        return (b, jnp.minimum(i, nct - 1), 0)

    def modrow(b, i):
        return (jnp.where(i < nct, bsz, b), 0, 0)

    def allk(b, i):
        return (b, i, 0)

    def tab(b, i):
        return (i, 0)

    bf = lambda w: jax.ShapeDtypeStruct((bsz, n, w), BF16)
    bfs = lambda w: jax.ShapeDtypeStruct((bsz, s, w), BF16)
    return pl.pallas_call(
        functools.partial(_proj_kernel, nct),
        out_shape=(bf(HA * 2 * DA), bf(HB * 256), bf(2 * d), bfs(HA * 2 * DA), bfs(HA * 256), bfs(HB * 256), bfs(HB * 256)),
        grid=(bsz, s // tn),
        in_specs=[
            pl.BlockSpec((1, tn, d), lat), pl.BlockSpec((1, tn, d), cx),
            pl.BlockSpec((1, 1, d), modrow), pl.BlockSpec((1, 1, d), modrow),
            _resident((d, ncols)), _full((1, Q_LORA)), _full((1, KV_LORA)),
            _resident(wq.shape), _resident(wkv.shape),
            pl.BlockSpec((tn, LANES), tab), pl.BlockSpec((tn, LANES), tab),
            pl.BlockSpec((tn, LANES), tab), pl.BlockSpec((tn, LANES), tab),
        ],
        out_specs=(
            pl.BlockSpec((1, tn, HA * 2 * DA), lat), pl.BlockSpec((1, tn, HB * 256), lat),
            pl.BlockSpec((1, tn, 2 * d), lat),
            pl.BlockSpec((1, tn, HA * 2 * DA), allk), pl.BlockSpec((1, tn, HA * 256), allk),
            pl.BlockSpec((1, tn, HB * 256), allk), pl.BlockSpec((1, tn, HB * 256), allk),
        ),
        compiler_params=_cparams(("parallel", "arbitrary")),
        name="projection",
    )(x, ctx, sc1, sh1, w_all, q_norm_g, kv_norm_g, wq, wkv, cf, sf, ch, shh)


_NT = (((1,), (1,)), ((), ()))


def _softmax_pv(q, k, v_ones):
    s = lax.dot_general(q, k, _NT, preferred_element_type=F32)
    m = jnp.max(s, axis=-1, keepdims=True)
    p = jnp.exp2(s - m).astype(BF16)
    o = jnp.dot(p, v_ones, preferred_element_type=F32)
    w = o.shape[1] // 2
    return o[:, :w] / o[:, w:w + 1]


def _mixers_kernel(qa_ref, ka_ref, va_ref, qm_ref, km_ref, vb_ref, lam_ref, g_ref, ya_ref, yb_ref):
    q = qa_ref[0]
    k = ka_ref[0]
    v = va_ref[0]
    lane = lax.broadcasted_iota(jnp.int32, q.shape, 1)
    zero = jnp.zeros_like(q)
    o1 = _softmax_pv(jnp.where(lane < DA, q, zero), k, v)
    o2 = _softmax_pv(jnp.where(lane < DA, zero, q), k, v)
    lv = lam_ref[...]
    lam = (jnp.exp(jnp.sum(lv[0:1] * lv[1:2], axis=-1, keepdims=True))
           - jnp.exp(jnp.sum(lv[2:3] * lv[3:4], axis=-1, keepdims=True)) + LAMBDA_INIT)
    o = o1 - lam * o2
    y = o * lax.rsqrt(jnp.mean(o * o, axis=-1, keepdims=True) + EPS) * g_ref[...]
    ya_ref[0] = (y * (1.0 - LAMBDA_INIT)).astype(BF16)
    yb_ref[0] = _softmax_pv(qm_ref[0], km_ref[0], vb_ref[0]).astype(BF16)


def _mixers(qa, ka, va, qm, km, vb, lam_rows, subln_g):
    assert HA == HB
    bsz, n, w = qa.shape
    s = ka.shape[1]
    tq = Q_TILE
    qspec = lambda width: pl.BlockSpec((1, tq, width), lambda b, h, i: (b, i, h))
    kspec = lambda width: pl.BlockSpec((1, s, width), lambda b, h, i: (b, 0, h))
    return pl.pallas_call(
        _mixers_kernel,
        out_shape=(jax.ShapeDtypeStruct((bsz, n, w), BF16), jax.ShapeDtypeStruct((bsz, n, HB * VB), BF16)),
        grid=(bsz, HA, n // tq),
        in_specs=[qspec(LANES), kspec(LANES), kspec(256), qspec(256), kspec(256), kspec(256),
                  _full(lam_rows.shape), _full(subln_g.shape)],
        out_specs=(qspec(LANES), qspec(VB)),
        compiler_params=_cparams(("parallel", "parallel", "arbitrary")),
        name="token_mixers",
    )(qa, ka, va, qm, km, vb, lam_rows, subln_g)


def _layer_norm(r, g, b):
    mu = jnp.mean(r, axis=-1, keepdims=True)
    c = r - mu
    var = jnp.mean(c * c, axis=-1, keepdims=True)
    return c * lax.rsqrt(var + EPS) * g + b


def _merge_kernel(ya_ref, yb_ref, gs_ref, x_ref, wpa_ref, wpb_ref, wo_ref, gt1_ref, sc2_ref, sh2_ref,
                  g1_ref, b1_ref, wpq_ref, keys_ref, x1_ref, h2_ref, sct_ref):
    d = x_ref.shape[-1]
    gs = gs_ref[...].astype(F32)
    za = jnp.dot(ya_ref[...], wpa_ref[...], preferred_element_type=F32)
    zb = jnp.dot(yb_ref[...], wpb_ref[...], preferred_element_type=F32)
    z = gs[:, :d] * za + gs[:, d:] * zb
    y = jnp.dot(z.astype(BF16), wo_ref[...], preferred_element_type=F32)
    x1 = _layer_norm(DEEPNORM_ALPHA * x_ref[...] + gt1_ref[0] * y, g1_ref[...], b1_ref[...])
    h2 = x1 * (1.0 + sc2_ref[0]) + sh2_ref[0]
    tn = x1.shape[0]
    for ch in range(d // LANES):
        x1_ref[pl.ds(ch, tn, stride=8), :] = x1[:, ch * LANES:(ch + 1) * LANES]
        h2_ref[pl.ds(ch, tn, stride=8), :] = h2[:, ch * LANES:(ch + 1) * LANES]
    qp = jnp.dot(h2.astype(BF16), wpq_ref[...], preferred_element_type=F32).astype(BF16)
    for hh in range(2 * HP):
        sct_ref[hh] = lax.dot_general(keys_ref[hh], qp[:, hh * DK_HALF:(hh + 1) * DK_HALF], _NT,
                                      preferred_element_type=F32)


def _merge(ya, yb, gs, x2d, wpa, wpb, wo, gt1, sc2, sh2, ln1_g, ln1_b, wpq, keys, tiles_per_batch):
    t, d = x2d.shape
    tn = TOKEN_TILE
    row = lambda i: (i, 0)
    brow = lambda i: (i // tiles_per_batch, 0, 0)
    return pl.pallas_call(
        _merge_kernel,
        out_shape=(jax.ShapeDtypeStruct((t * 8, LANES), F32), jax.ShapeDtypeStruct((t * 8, LANES), F32),
                   jax.ShapeDtypeStruct((2 * HP, N_KEYS, t), F32)),
        grid=(t // tn,),
        in_specs=[
            pl.BlockSpec((tn, d), row), pl.BlockSpec((tn, d), row), pl.BlockSpec((tn, 2 * d), row),
            pl.BlockSpec((tn, d), row),
            _resident(wpa.shape), _resident(wpb.shape), _resident(wo.shape),
            pl.BlockSpec((1, 1, d), brow), pl.BlockSpec((1, 1, d), brow), pl.BlockSpec((1, 1, d), brow),
            _full((1, d)), _full((1, d)), _resident(wpq.shape), _resident(keys.shape),
        ],
        out_specs=(pl.BlockSpec((tn * 8, LANES), row), pl.BlockSpec((tn * 8, LANES), row),
                   pl.BlockSpec((2 * HP, N_KEYS, tn), lambda i: (0, 0, i))),
        compiler_params=_cparams(("parallel",)),
        name="merge",
    )(ya, yb, gs, x2d, wpa, wpb, wo, gt1, sc2, sh2, ln1_g, ln1_b, wpq, keys)


SUBLANES = 8


def _tree(items, merge):
    while len(items) > 1:
        nxt = [merge(items[k], items[k + 1]) for k in range(0, len(items) - 1, 2)]
        if len(items) % 2:
            nxt.append(items[-1])
        items = nxt
    return items[0]


def _all_sublanes(x, op):
    for sh in (4, 2, 1):
        x = op(x, pltpu.roll(x, sh, axis=0))
    return x


def _top_keys(s_ref, k):
    nblk = s_ref.shape[0] // SUBLANES
    tl = s_ref.shape[1]
    sub = lax.broadcasted_iota(jnp.int32, (SUBLANES, tl), 0)
    rows = [sub + SUBLANES * b for b in range(nblk)]
    vs = [s_ref[SUBLANES * b:SUBLANES * (b + 1), :] for b in range(nblk)]

    def merge(a, b):
        c = a[0] >= b[0]
        return jnp.where(c, a[0], b[0]), jnp.where(c, a[1], b[1])

    vals, idxs = [], []
    for _ in range(k):
        v, i = _tree(list(zip(vs, rows)), merge)
        for sh in (4, 2, 1):
            pv, pi = pltpu.roll(v, sh, axis=0), pltpu.roll(i, sh, axis=0)
            c = (v > pv) | ((v == pv) & (i < pi))
            v, i = jnp.where(c, v, pv), jnp.where(c, i, pi)
        vals.append(v[0:1])
        idxs.append(i[0:1])
        vs = [jnp.where(rows[b] == i, -jnp.inf, vs[b]) for b in range(nblk)]
    return jnp.concatenate(vals, axis=0), jnp.concatenate(idxs, axis=0)


_CAND_BLOCKS = (("col", 0, 0), ("col", 0, 8), ("col", 1, 0),
                ("row", 0, 0, 2, 7), ("row", 0, 8, 8, 15), ("row", 1, 0, 2, 7),
                ("row", 2, 0, 2, 4), ("row", 3, 0, 2, 3), ("row", 4, 0, 2, 2))
assert TOPK == 16 and TOPK_HALF == 16 and sum(
    SUBLANES if blk[0] == "col" else blk[4] - blk[3] + 1 for blk in _CAND_BLOCKS) == sum(
    1 for a in range(16) for b in range(16) if (a + 1) * (b + 1) <= TOPK)


def _top_pairs(v1, i1, v2, i2, k):
    tl = v1.shape[1]
    sub = lax.broadcasted_iota(jnp.int32, (SUBLANES, tl), 0)
    big = TOPK_HALF * TOPK_HALF
    cs, flats, pays = [], [], []
    for blk in _CAND_BLOCKS:
        if blk[0] == "col":
            _, b, a0 = blk
            cs.append(v1[a0:a0 + SUBLANES] + v2[b:b + 1])
            pays.append((i1[a0:a0 + SUBLANES] * N_KEYS + i2[b:b + 1]) * SLAB_ROWS)
            flats.append((sub + a0) * TOPK_HALF + b)
        else:
            _, a, b0, lo, hi = blk
            ok = (sub + b0 >= lo) & (sub + b0 <= hi)
            cs.append(jnp.where(ok, v1[a:a + 1] + v2[b0:b0 + SUBLANES], -jnp.inf))
            pays.append((i1[a:a + 1] * N_KEYS + i2[b0:b0 + SUBLANES]) * SLAB_ROWS)
            flats.append(jnp.where(ok, a * TOPK_HALF + sub + b0, big))
    vals, outs = [], []
    for _ in range(k):
        m = _all_sublanes(_tree(cs, jnp.maximum), jnp.maximum)
        pos = _all_sublanes(_tree([jnp.where(c == m, f, big) for c, f in zip(cs, flats)], jnp.minimum), jnp.minimum)
        sels = [f == pos for f in flats]
        pay = _all_sublanes(_tree([jnp.where(s, p, -1) for s, p in zip(sels, pays)], jnp.maximum), jnp.maximum)
        vals.append(m[0:1])
        outs.append(pay[0:1])
        cs = [jnp.where(s, -jnp.inf, c) for s, c in zip(sels, cs)]
    return jnp.concatenate(vals, axis=0), jnp.concatenate(outs, axis=0)


def _route_kernel(sct_ref, idx_ref, g_ref, idxt_ref):
    def head(hh, carry):
        v1, i1 = _top_keys(sct_ref.at[2 * hh], TOPK_HALF)
        v2, i2 = _top_keys(sct_ref.at[2 * hh + 1], TOPK_HALF)
        gv, gi = _top_pairs(v1, i1, v2, i2, TOPK)
        e = jnp.exp(gv - gv[0:1])
        rows = pl.ds(pl.multiple_of(hh * TOPK, TOPK), TOPK)
        g_ref[rows, :] = e / jnp.sum(e, axis=0, keepdims=True)
        idxt_ref[rows, :] = gi.astype(F32)
        return carry

    lax.fori_loop(0, HP, head, 0)
    idx_ref[...] = jnp.transpose(idxt_ref[...]).astype(jnp.int32)


def _routing(sct):
    _, _, t = sct.shape
    tl = ROUTE_TILE
    nsel = HP * TOPK
    return pl.pallas_call(
        _route_kernel,
        out_shape=(jax.ShapeDtypeStruct((t, nsel), jnp.int32), jax.ShapeDtypeStruct((nsel, t), F32)),
        grid=(t // tl,),
        in_specs=[pl.BlockSpec((2 * HP, N_KEYS, tl), lambda i: (0, 0, i))],
        out_specs=(pl.BlockSpec((tl, nsel), lambda i: (i, 0)), pl.BlockSpec((nsel, tl), lambda i: (0, i))),
        scratch_shapes=[pltpu.VMEM((nsel, tl), F32)],
        compiler_params=_cparams(("parallel",)),
        name="routing",
    )(sct)


SLAB_ROWS = 4
_HI_MASK = 0xFFFF0000


def _unpack(slab):
    lo = lax.bitcast_convert_type(slab << 16, F32)
    hi = lax.bitcast_convert_type(slab & jnp.uint32(_HI_MASK), F32)
    return lo, hi


def _gather_slabs(idx_ref, tab_ref, slab_ref, t):
    group = 8
    for j0 in range(0, idx_ref.shape[1], group):
        row = idx_ref.at[t, pl.ds(j0, group)]
        for k in range(group):
            j = j0 + k
            first = pl.multiple_of(row[k], SLAB_ROWS)
            slab_ref[SLAB_ROWS * j:SLAB_ROWS * (j + 1), :] = tab_ref[pl.ds(first, SLAB_ROWS), :]


def _two_phase_loop(tb, gather, math, init):
    gather(0, 0)
    gather(1, 1)

    def two_tokens(i, carry):
        t0 = 2 * i
        carry = math(0, t0, carry)
        carry = math(1, t0 + 1, carry)
        gather(0, jnp.minimum(t0 + 2, tb - 1))
        gather(1, jnp.minimum(t0 + 3, tb - 1))
        return carry

    return lax.fori_loop(0, tb // 2, two_tokens, init)


def _slab_rows(slab_ref, s, nsel):
    return slab_ref[pl.ds(s, nsel, stride=4), :]


def _expert_down_kernel(idx_ref, x_ref, g_ref, u_ref, w_ref, slab0_ref, slab1_ref):
    tb = x_ref.shape[0]
    nsel = idx_ref.shape[1]
    slabs = (slab0_ref, slab1_ref)
    lane = lax.broadcasted_iota(jnp.int32, (nsel, tb), 1)

    def gather(buf, t):
        _gather_slabs(idx_ref, u_ref, slabs[buf], t)

    def math(buf, t, at):
        xv = x_ref[t]
        acc = None
        for s in range(SLAB_ROWS):
            lo, hi = _unpack(_slab_rows(slabs[buf], s, nsel))
            term = lo * xv[s:s + 1] + hi * xv[SLAB_ROWS + s:SLAB_ROWS + s + 1]
            acc = term if acc is None else acc + term
        a = jnp.sum(acc, axis=1, keepdims=True)
        return jnp.where(lane == t, a, at)

    at = _two_phase_loop(tb, gather, math, jnp.zeros((nsel, tb), F32))
    w_ref[...] = jax.nn.gelu(at) * g_ref[...]


def _expert_down(idx, x3, g_t, u_tab):
    t = x3.shape[0]
    tb = EXPERT_TILE
    nsel = idx.shape[1]
    return pl.pallas_call(
        _expert_down_kernel,
        out_shape=jax.ShapeDtypeStruct((nsel, t), F32),
        grid=(t // tb,),
        in_specs=[
            pl.BlockSpec((tb, nsel), lambda i: (i, 0), memory_space=pltpu.SMEM),
            pl.BlockSpec((tb, 8, LANES), lambda i: (i, 0, 0)),
            pl.BlockSpec((nsel, tb), lambda i: (0, i)),
            _resident(u_tab.shape),
        ],
        out_specs=pl.BlockSpec((nsel, tb), lambda i: (0, i)),
        scratch_shapes=[pltpu.VMEM((4 * nsel, LANES), jnp.uint32), pltpu.VMEM((4 * nsel, LANES), jnp.uint32)],
        compiler_params=_cparams(("arbitrary",)),
        name="expert_down",
    )(idx, x3, g_t, u_tab)


def _expert_up_kernel(idx_ref, wt_ref, x1_ref, v_ref, gt2_ref, g2_ref, b2_ref, o_ref, acc_ref, y_ref,
                      slab0_ref, slab1_ref):
    tb = x1_ref.shape[0]
    nsel = idx_ref.shape[1]
    slabs = (slab0_ref, slab1_ref)

    def gather(buf, t):
        _gather_slabs(idx_ref, v_ref, slabs[buf], t)

    def math(buf, t, carry):
        wt = wt_ref[...]
        wb = jnp.take_along_axis(wt, jnp.full(wt.shape, t, jnp.int32), axis=1, mode="promise_in_bounds")
        los, his = [], []
        for s in range(SLAB_ROWS):
            lo, hi = _unpack(_slab_rows(slabs[buf], s, nsel))
            los.append(jnp.sum(lo * wb, axis=0, keepdims=True))
            his.append(jnp.sum(hi * wb, axis=0, keepdims=True))
        acc_ref[t] = jnp.concatenate(los + his, axis=0)
        return carry

    _two_phase_loop(tb, gather, math, 0)
    r = DEEPNORM_ALPHA * x1_ref[...] + gt2_ref[...] * acc_ref[...]
    n = r.shape[1] * r.shape[2]
    mu = jnp.sum(jnp.sum(r, axis=2, keepdims=True), axis=1, keepdims=True) / n
    c = r - mu
    var = jnp.sum(jnp.sum(c * c, axis=2, keepdims=True), axis=1, keepdims=True) / n
    y_ref[...] = (c * lax.rsqrt(var + EPS) * g2_ref[...] + b2_ref[...]).reshape(tb * 8, LANES)
    for ch in range(8):
        o_ref[:, ch * LANES:(ch + 1) * LANES] = y_ref[pl.ds(ch, tb, stride=8), :]


def _expert_up(idx, w_t, x13, v_tab, gt2, ln2_g, ln2_b, tiles_per_batch):
    t = x13.shape[0]
    tb = EXPERT_TILE
    nsel = idx.shape[1]
    tok = pl.BlockSpec((tb, 8, LANES), lambda i: (i, 0, 0))
    return pl.pallas_call(
        _expert_up_kernel,
        out_shape=jax.ShapeDtypeStruct((t, 8 * LANES), F32),
        grid=(t // tb,),
        in_specs=[pl.BlockSpec((tb, nsel), lambda i: (i, 0), memory_space=pltpu.SMEM),
                  pl.BlockSpec((nsel, tb), lambda i: (0, i)), tok, _resident(v_tab.shape),
                  pl.BlockSpec((1, 8, LANES), lambda i: (i // tiles_per_batch, 0, 0)),
                  _full((1, 8, LANES)), _full((1, 8, LANES))],
        out_specs=pl.BlockSpec((tb, 8 * LANES), lambda i: (i, 0)),
        scratch_shapes=[pltpu.VMEM((tb, 8, LANES), F32), pltpu.VMEM((tb * 8, LANES), F32),
                        pltpu.VMEM((4 * nsel, LANES), jnp.uint32), pltpu.VMEM((4 * nsel, LANES), jnp.uint32)],
        compiler_params=_cparams(("arbitrary",)),
        name="expert_up",
    )(idx, w_t, x13, v_tab, gt2, ln2_g, ln2_b)


def _pack_table(tab):
    e, d = tab.shape
    bits = lax.bitcast_convert_type(tab.astype(BF16), jnp.uint16).astype(jnp.uint32)
    half = d // 2
    word = bits[:, :half] | (bits[:, half:] << 16)
    assert half == SLAB_ROWS * LANES
    return word.reshape(e * SLAB_ROWS, LANES)


def _rope_tables(n, nctx):
    rows = n // GRID_W
    quarter = DA // 4
    inv = ROPE_BASE ** (-jnp.arange(quarter, dtype=F32) / quarter)
    row_ang = jnp.arange(rows, dtype=F32)[:, None] * inv
    col_ang = jnp.arange(GRID_W, dtype=F32)[:, None] * inv
    ang = jnp.concatenate([
        jnp.broadcast_to(row_ang[:, None, :], (rows, GRID_W, quarter)),
        jnp.broadcast_to(col_ang[None, :, :], (rows, GRID_W, quarter))], -1).reshape(n, 2 * quarter)
    cos = jnp.concatenate([jnp.ones((nctx, 2 * quarter), F32), jnp.cos(ang)], 0)
    sin = jnp.concatenate([jnp.zeros((nctx, 2 * quarter), F32), jnp.sin(ang)], 0)
    one, zero = jnp.ones_like(cos), jnp.zeros_like(sin)
    cf = jnp.concatenate([cos, cos, cos, cos], 1)
    sf = jnp.concatenate([-sin, sin, -sin, sin], 1)
    ch = jnp.concatenate([cos, cos, one, one], 1)
    sh = jnp.concatenate([-sin, sin, zero, zero], 1)
    return cf, sf, ch, sh


def kernel(x, c, ctx, c_ctx, w_mod, b_mod, w_in, w_q_up, q_norm_g, w_kv_up, kv_norm_g, lambda_q1, lambda_k1,
           lambda_q2, lambda_k2, subln_g, w_pa, w_pb, w_out, ln1_g, ln1_b, w_pq, peer_keys, peer_u, peer_v,
           ln2_g, ln2_b):
    bsz, n, d = x.shape
    nctx = ctx.shape[1]
    t = bsz * n
    assert d == 8 * LANES and n % TOKEN_TILE == 0 and nctx % TOKEN_TILE == 0 and n % GRID_W == 0
    assert t % ROUTE_TILE == 0 and n % EXPERT_TILE == 0 and w_mod.shape[0] == DEPTH

    rows = -(-(bsz + 1) // 8) * 8
    c_rows = jnp.concatenate([c, c_ctx[None, :], jnp.zeros((rows - bsz - 1, d), F32)], 0)
    mod = _modulation(c_rows, w_mod[0], b_mod[0][None, :])[:bsz + 1]
    sh1, sc1, gt1, sh2, sc2, gt2 = [m[:, None, :] for m in jnp.split(mod, 6, axis=-1)]

    w_all = jnp.concatenate([w_in[0], jnp.zeros((d, LANES - ROPE), F32)], 1).astype(BF16)
    wq = w_q_up[0].reshape(Q_LORA, HB, NOPE + ROPE)
    wq = jnp.concatenate([wq, jnp.zeros((Q_LORA, HB, 256 - NOPE - ROPE), F32)], -1).reshape(Q_LORA, HB * 256)
    cf, sf, ch, shh = _rope_tables(n, nctx)
    lam_rows = jnp.zeros((8, LANES), F32).at[0:4, 0:DA].set(
        jnp.stack([lambda_q1[0], lambda_k1[0], lambda_q2[0], lambda_k2[0]]))

    qa, qm, gs, ka, va, km, vb = _projection(
        x, ctx, sc1, sh1, w_all, q_norm_g[0][None, :], kv_norm_g[0][None, :], wq.astype(BF16),
        w_kv_up[0].astype(BF16), cf, sf, ch, shh)

    ya, yb = _mixers(qa, ka, va, qm, km, vb, lam_rows, subln_g[0][None, :])

    keys = peer_keys[0].reshape(2 * HP, N_KEYS, DK_HALF).astype(BF16)
    x1, h2, sct = _merge(
        ya.reshape(t, d), yb.reshape(t, d), gs.reshape(t, 2 * d), x.reshape(t, d),
        w_pa[0].astype(BF16), w_pb[0].astype(BF16), w_out[0].astype(BF16),
        gt1[:bsz], sc2[:bsz], sh2[:bsz], ln1_g[0][None, :], ln1_b[0][None, :],
        w_pq[0].astype(BF16), keys, n // TOKEN_TILE)

    idx, g_t = _routing(sct)

    w_t = _expert_down(idx, h2.reshape(t, 8, LANES), g_t, _pack_table(peer_u[0]))
    out = _expert_up(idx, w_t, x1.reshape(t, 8, LANES), _pack_table(peer_v[0]),
                     gt2[:bsz].reshape(bsz, 8, LANES), ln2_g[0].reshape(1, 8, LANES),
                     ln2_b[0].reshape(1, 8, LANES), n // EXPERT_TILE)
    return out.reshape(bsz, n, d)
```

```python
import functools
import math

import jax
import jax.numpy as jnp
from jax import lax
from jax.experimental import pallas as pl
from jax.experimental.pallas import tpu as pltpu

GRID_W = 64
HA, DA = 8, 64
HB, NOPE, ROPE, VB = 8, 128, 64, 128
Q_LORA, KV_LORA = 256, 256
N_KEYS = 128
HP, DK_HALF = 8, 128
TOPK_HALF, TOPK = 16, 16
ROPE_BASE = 10000.0
EPS = 1e-6
DEPTH = 1
DEEPNORM_ALPHA = (2 * DEPTH) ** 0.25
LAMBDA_INIT = 0.8 - 0.6 * math.exp(-0.3 * 0)
LOG2E = 1.0 / math.log(2.0)
QA_SCALE = DA ** -0.5 * LOG2E
QM_SCALE = (NOPE + ROPE) ** -0.5 * LOG2E

LANES = 128
VMEM_LIMIT_BYTES = 56 * 1024 * 1024

TOKEN_TILE = 256
Q_TILE = 256
ROUTE_TILE = 512
EXPERT_TILE = 128

F32 = jnp.float32
BF16 = jnp.bfloat16


def _cparams(sem):
    return pltpu.CompilerParams(dimension_semantics=sem, vmem_limit_bytes=VMEM_LIMIT_BYTES)


def _full(shape):
    n = len(shape)
    return pl.BlockSpec(shape, lambda *_: (0,) * n)


def _resident(shape):
    n = len(shape)
    return pl.BlockSpec(shape, lambda *_: (0,) * n, pipeline_mode=pl.Buffered(1))


def _mod_kernel(c_ref, w_ref, b_ref, o_ref):
    c = c_ref[...]
    s = c * jax.nn.sigmoid(c)
    o_ref[...] = jnp.dot(s, w_ref[...], preferred_element_type=F32) + b_ref[...]


def _modulation(c_rows, w_mod, b_mod):
    r, d = c_rows.shape
    n = w_mod.shape[1]
    tn = 1536
    return pl.pallas_call(
        _mod_kernel,
        out_shape=jax.ShapeDtypeStruct((r, n), F32),
        grid=(n // tn,),
        in_specs=[_full((r, d)), pl.BlockSpec((d, tn), lambda j: (0, j)), pl.BlockSpec((1, tn), lambda j: (0, j))],
        out_specs=pl.BlockSpec((r, tn), lambda j: (0, j)),
        compiler_params=_cparams(("arbitrary",)),
        name="modulation",
    )(c_rows, w_mod, b_mod)


def _swap_halves(blk):
    lane = lax.broadcasted_iota(jnp.int32, blk.shape, 1)
    up = pltpu.roll(blk, LANES - 32, axis=1)
    dn = pltpu.roll(blk, 32, axis=1)
    return jnp.where((lane % 64) < 32, up, dn)


def _rope_block(blk, c, s):
    return blk * c + _swap_halves(blk) * s


def _rms(x, g):
    return x * lax.rsqrt(jnp.mean(x * x, axis=-1, keepdims=True) + EPS) * g


def _proj_kernel(nct, x_ref, ctx_ref, sc_ref, sh_ref, w_ref, qg_ref, kvg_ref, wq_ref, wkv_ref,
                 cf_ref, sf_ref, ch_ref, shh_ref,
                 qa_ref, qm_ref, gs_ref, ka_ref, va_ref, km_ref, vb_ref):
    i = pl.program_id(1)
    d = x_ref.shape[-1]
    xt = jnp.where(i < nct, ctx_ref[0], x_ref[0])
    h = xt * (1.0 + sc_ref[0]) + sh_ref[0]
    hb = h.astype(BF16)
    cf, sf = cf_ref[...], sf_ref[...]
    ch, shh = ch_ref[...], shh_ref[...]

    def proj(lo, hi):
        return jnp.dot(hb, w_ref[:, lo:hi], preferred_element_type=F32)

    c_qa, c_ql, c_g = 0, HA * 2 * DA, HA * 2 * DA + Q_LORA
    c_ka = c_g + 2 * d
    c_va = c_ka + HA * 2 * DA
    c_kvl = c_va + HA * 2 * DA
    c_kr = c_kvl + KV_LORA

    ka = proj(c_ka, c_va)
    for b in range(ka.shape[1] // LANES):
        sl = slice(b * LANES, (b + 1) * LANES)
        ka_ref[0, :, sl] = _rope_block(ka[:, sl], cf, sf).astype(BF16)
    lane = lax.broadcasted_iota(jnp.int32, (hb.shape[0], LANES), 1)
    ones_col = jnp.where(lane == 0, 1.0, 0.0).astype(BF16)
    va = proj(c_va, c_kvl).astype(BF16)
    for hh in range(HA):
        va_ref[0, :, hh * 256:hh * 256 + 2 * DA] = va[:, hh * 2 * DA:(hh + 1) * 2 * DA]
        va_ref[0, :, hh * 256 + 2 * DA:(hh + 1) * 256] = ones_col
    kvl = _rms(proj(c_kvl, c_kr), kvg_ref[...]).astype(BF16)
    kv = jnp.dot(kvl, wkv_ref[...], preferred_element_type=F32)
    kr = _rope_block(proj(c_kr, c_kr + LANES), ch, shh).astype(BF16)
    for hh in range(HB):
        base = hh * (NOPE + VB)
        km_ref[0, :, hh * 256:hh * 256 + NOPE] = kv[:, base:base + NOPE].astype(BF16)
        km_ref[0, :, hh * 256 + NOPE:(hh + 1) * 256] = kr
        vb_ref[0, :, hh * 256:hh * 256 + VB] = kv[:, base + NOPE:base + NOPE + VB].astype(BF16)
        vb_ref[0, :, hh * 256 + VB:(hh + 1) * 256] = ones_col

    @pl.when(i >= nct)
    def _():
        qa = proj(c_qa, c_ql)
        for b in range(qa.shape[1] // LANES):
            sl = slice(b * LANES, (b + 1) * LANES)
            qa_ref[0, :, sl] = (_rope_block(qa[:, sl], cf, sf) * QA_SCALE).astype(BF16)
        ql = _rms(proj(c_ql, c_g), qg_ref[...]).astype(BF16)
        qb = jnp.dot(ql, wq_ref[...], preferred_element_type=F32) * QM_SCALE
        for hh in range(HB):
            qm_ref[0, :, hh * 256:hh * 256 + NOPE] = qb[:, hh * 256:hh * 256 + NOPE].astype(BF16)
            qm_ref[0, :, hh * 256 + NOPE:(hh + 1) * 256] = _rope_block(
                qb[:, hh * 256 + NOPE:(hh + 1) * 256], ch, shh).astype(BF16)
        gs_ref[0] = jax.nn.sigmoid(proj(c_g, c_ka)).astype(BF16)


def _projection(x, ctx, sc1, sh1, w_all, q_norm_g, kv_norm_g, wq, wkv, cf, sf, ch, shh):
    bsz, n, d = x.shape
    nctx = ctx.shape[1]
    tn = TOKEN_TILE
    nct = nctx // tn
    s = nctx + n
    ncols = w_all.shape[1]

    def lat(b, i):
        return (b, jnp.maximum(i - nct, 0), 0)

    def cx(b, i):
        return (b, jnp.minimum(i, nct - 1), 0)

    def modrow(b, i):
        return (jnp.where(i < nct, bsz, b), 0, 0)

    def allk(b, i):
        return (b, i, 0)

    def tab(b, i):
        return (i, 0)

    bf = lambda w: jax.ShapeDtypeStruct((bsz, n, w), BF16)
    bfs = lambda w: jax.ShapeDtypeStruct((bsz, s, w), BF16)
    return pl.pallas_call(
        functools.partial(_proj_kernel, nct),
        out_shape=(bf(HA * 2 * DA), bf(HB * 256), bf(2 * d), bfs(HA * 2 * DA), bfs(HA * 256), bfs(HB * 256), bfs(HB * 256)),
        grid=(bsz, s // tn),
        in_specs=[
            pl.BlockSpec((1, tn, d), lat), pl.BlockSpec((1, tn, d), cx),
            pl.BlockSpec((1, 1, d), modrow), pl.BlockSpec((1, 1, d), modrow),
            _resident((d, ncols)), _full((1, Q_LORA)), _full((1, KV_LORA)),
            _resident(wq.shape), _resident(wkv.shape),
            pl.BlockSpec((tn, LANES), tab), pl.BlockSpec((tn, LANES), tab),
            pl.BlockSpec((tn, LANES), tab), pl.BlockSpec((tn, LANES), tab),
        ],
        out_specs=(
            pl.BlockSpec((1, tn, HA * 2 * DA), lat), pl.BlockSpec((1, tn, HB * 256), lat),
            pl.BlockSpec((1, tn, 2 * d), lat),
            pl.BlockSpec((1, tn, HA * 2 * DA), allk), pl.BlockSpec((1, tn, HA * 256), allk),
            pl.BlockSpec((1, tn, HB * 256), allk), pl.BlockSpec((1, tn, HB * 256), allk),
        ),
        compiler_params=_cparams(("parallel", "arbitrary")),
        name="projection",
    )(x, ctx, sc1, sh1, w_all, q_norm_g, kv_norm_g, wq, wkv, cf, sf, ch, shh)


_NT = (((1,), (1,)), ((), ()))


def _probs(q, k_ref, p_ref):
    s = lax.dot_general(q, k_ref[0], _NT, preferred_element_type=F32)
    p_ref[...] = jnp.exp2(s - jnp.max(s, axis=-1, keepdims=True)).astype(BF16)


def _weighted_values(p_ref, v_ref):
    o = jnp.dot(p_ref[...], v_ref[0], preferred_element_type=F32)
    w = o.shape[1] // 2
    return o[:, :w] / o[:, w:w + 1]


def _mixers_kernel(qa_ref, qa_next_ref, ka_ref, va_ref, qm_ref, qm_next_ref, km_ref, vb_ref, lam_ref, g_ref,
                   ya_ref, yb_ref, p_even_ref, p_odd_ref):
    i = pl.program_id(2)

    def fill(p_ref, qa, qm):
        lane = lax.broadcasted_iota(jnp.int32, qa.shape, 1)
        zero = jnp.zeros_like(qa)
        _probs(jnp.where(lane < DA, qa, zero), ka_ref, p_ref.at[0])
        _probs(jnp.where(lane < DA, zero, qa), ka_ref, p_ref.at[1])
        _probs(qm, km_ref, p_ref.at[2])

    def finish(p_ref):
        lv = lam_ref[...]
        lam = (jnp.exp(jnp.sum(lv[0:1] * lv[1:2], axis=-1, keepdims=True))
               - jnp.exp(jnp.sum(lv[2:3] * lv[3:4], axis=-1, keepdims=True)) + LAMBDA_INIT)
        o = _weighted_values(p_ref.at[0], va_ref) - lam * _weighted_values(p_ref.at[1], va_ref)
        y = o * lax.rsqrt(jnp.mean(o * o, axis=-1, keepdims=True) + EPS) * g_ref[...]
        ya_ref[0] = (y * (1.0 - LAMBDA_INIT)).astype(BF16)
        yb_ref[0] = _weighted_values(p_ref.at[2], vb_ref).astype(BF16)

    @pl.when(i == 0)
    def _():
        fill(p_even_ref, qa_ref[0], qm_ref[0])

    @pl.when(i % 2 == 0)
    def _():
        fill(p_odd_ref, qa_next_ref[0], qm_next_ref[0])
        finish(p_even_ref)

    @pl.when(i % 2 == 1)
    def _():
        fill(p_even_ref, qa_next_ref[0], qm_next_ref[0])
        finish(p_odd_ref)


def _mixers(qa, ka, va, qm, km, vb, lam_rows, subln_g):
    assert HA == HB
    bsz, n, w = qa.shape
    s = ka.shape[1]
    tq = Q_TILE
    last = n // tq - 1
    qspec = lambda width: pl.BlockSpec((1, tq, width), lambda b, h, i: (b, i, h))
    qnext = lambda width: pl.BlockSpec((1, tq, width), lambda b, h, i: (b, jnp.minimum(i + 1, last), h))
    kspec = lambda width: pl.BlockSpec((1, s, width), lambda b, h, i: (b, 0, h))
    return pl.pallas_call(
        _mixers_kernel,
        out_shape=(jax.ShapeDtypeStruct((bsz, n, w), BF16), jax.ShapeDtypeStruct((bsz, n, HB * VB), BF16)),
        grid=(bsz, HA, n // tq),
        in_specs=[qspec(LANES), qnext(LANES), kspec(LANES), kspec(256), qspec(256), qnext(256), kspec(256),
                  kspec(256), _full(lam_rows.shape), _full(subln_g.shape)],
        out_specs=(qspec(LANES), qspec(VB)),
        scratch_shapes=[pltpu.VMEM((3, tq, s), BF16), pltpu.VMEM((3, tq, s), BF16)],
        compiler_params=_cparams(("arbitrary", "arbitrary", "arbitrary")),
        name="token_mixers",
    )(qa, qa, ka, va, qm, qm, km, vb, lam_rows, subln_g)


def _layer_norm(r, g, b):
    mu = jnp.mean(r, axis=-1, keepdims=True)
    c = r - mu
    var = jnp.mean(c * c, axis=-1, keepdims=True)
    return c * lax.rsqrt(var + EPS) * g + b


def _merge_kernel(ya_ref, yb_ref, gs_ref, x_ref, wpa_ref, wpb_ref, wo_ref, gt1_ref, sc2_ref, sh2_ref,
                  g1_ref, b1_ref, wpq_ref, keys_ref, x1_ref, h2_ref, sct_ref):
    d = x_ref.shape[-1]
    gs = gs_ref[...].astype(F32)
    za = jnp.dot(ya_ref[...], wpa_ref[...], preferred_element_type=F32)
    zb = jnp.dot(yb_ref[...], wpb_ref[...], preferred_element_type=F32)
    z = gs[:, :d] * za + gs[:, d:] * zb
    y = jnp.dot(z.astype(BF16), wo_ref[...], preferred_element_type=F32)
    x1 = _layer_norm(DEEPNORM_ALPHA * x_ref[...] + gt1_ref[0] * y, g1_ref[...], b1_ref[...])
    h2 = x1 * (1.0 + sc2_ref[0]) + sh2_ref[0]
    tn = x1.shape[0]
    for ch in range(d // LANES):
        x1_ref[pl.ds(ch, tn, stride=8), :] = x1[:, ch * LANES:(ch + 1) * LANES]
        h2_ref[pl.ds(ch, tn, stride=8), :] = h2[:, ch * LANES:(ch + 1) * LANES]
    qp = jnp.dot(h2.astype(BF16), wpq_ref[...], preferred_element_type=F32).astype(BF16)
    for hh in range(2 * HP):
        sct_ref[hh] = lax.dot_general(keys_ref[hh], qp[:, hh * DK_HALF:(hh + 1) * DK_HALF], _NT,
                                      preferred_element_type=F32)


def _merge(ya, yb, gs, x2d, wpa, wpb, wo, gt1, sc2, sh2, ln1_g, ln1_b, wpq, keys, tiles_per_batch):
    t, d = x2d.shape
    tn = TOKEN_TILE
    row = lambda i: (i, 0)
    brow = lambda i: (i // tiles_per_batch, 0, 0)
    return pl.pallas_call(
        _merge_kernel,
        out_shape=(jax.ShapeDtypeStruct((t * 8, LANES), F32), jax.ShapeDtypeStruct((t * 8, LANES), F32),
                   jax.ShapeDtypeStruct((2 * HP, N_KEYS, t), F32)),
        grid=(t // tn,),
        in_specs=[
            pl.BlockSpec((tn, d), row), pl.BlockSpec((tn, d), row), pl.BlockSpec((tn, 2 * d), row),
            pl.BlockSpec((tn, d), row),
            _resident(wpa.shape), _resident(wpb.shape), _resident(wo.shape),
            pl.BlockSpec((1, 1, d), brow), pl.BlockSpec((1, 1, d), brow), pl.BlockSpec((1, 1, d), brow),
            _full((1, d)), _full((1, d)), _resident(wpq.shape), _resident(keys.shape),
        ],
        out_specs=(pl.BlockSpec((tn * 8, LANES), row), pl.BlockSpec((tn * 8, LANES), row),
                   pl.BlockSpec((2 * HP, N_KEYS, tn), lambda i: (0, 0, i))),
        compiler_params=_cparams(("parallel",)),
        name="merge",
    )(ya, yb, gs, x2d, wpa, wpb, wo, gt1, sc2, sh2, ln1_g, ln1_b, wpq, keys)


SUBLANES = 8


def _tree(items, merge):
    while len(items) > 1:
        nxt = [merge(items[k], items[k + 1]) for k in range(0, len(items) - 1, 2)]
        if len(items) % 2:
            nxt.append(items[-1])
        items = nxt
    return items[0]


def _all_sublanes(x, op):
    for sh in (4, 2, 1):
        x = op(x, pltpu.roll(x, sh, axis=0))
    return x


def _top_keys(s_ref, k):
    nblk = s_ref.shape[0] // SUBLANES
    tl = s_ref.shape[1]
    sub = lax.broadcasted_iota(jnp.int32, (SUBLANES, tl), 0)
    rows = [sub + SUBLANES * b for b in range(nblk)]
    vs = [s_ref[SUBLANES * b:SUBLANES * (b + 1), :] for b in range(nblk)]

    def merge(a, b):
        c = a[0] >= b[0]
        return jnp.where(c, a[0], b[0]), jnp.where(c, a[1], b[1])

    vals, idxs = [], []
    for _ in range(k):
        v, i = _tree(list(zip(vs, rows)), merge)
        for sh in (4, 2, 1):
            pv, pi = pltpu.roll(v, sh, axis=0), pltpu.roll(i, sh, axis=0)
            c = (v > pv) | ((v == pv) & (i < pi))
            v, i = jnp.where(c, v, pv), jnp.where(c, i, pi)
        vals.append(v[0:1])
        idxs.append(i[0:1])
        vs = [jnp.where(rows[b] == i, -jnp.inf, vs[b]) for b in range(nblk)]
    return jnp.concatenate(vals, axis=0), jnp.concatenate(idxs, axis=0)


_CAND_BLOCKS = (("col", 0, 0), ("col", 0, 8), ("col", 1, 0),
                ("row", 0, 0, 2, 7), ("row", 0, 8, 8, 15), ("row", 1, 0, 2, 7),
                ("row", 2, 0, 2, 4), ("row", 3, 0, 2, 3), ("row", 4, 0, 2, 2))
assert TOPK == 16 and TOPK_HALF == 16 and sum(
    SUBLANES if blk[0] == "col" else blk[4] - blk[3] + 1 for blk in _CAND_BLOCKS) == sum(
    1 for a in range(16) for b in range(16) if (a + 1) * (b + 1) <= TOPK)


def _top_pairs(v1, i1, v2, i2, k):
    tl = v1.shape[1]
    sub = lax.broadcasted_iota(jnp.int32, (SUBLANES, tl), 0)
    big = TOPK_HALF * TOPK_HALF
    cs, flats, pays = [], [], []
    for blk in _CAND_BLOCKS:
        if blk[0] == "col":
            _, b, a0 = blk
            cs.append(v1[a0:a0 + SUBLANES] + v2[b:b + 1])
            pays.append((i1[a0:a0 + SUBLANES] * N_KEYS + i2[b:b + 1]) * SLAB_ROWS)
            flats.append((sub + a0) * TOPK_HALF + b)
        else:
            _, a, b0, lo, hi = blk
            ok = (sub + b0 >= lo) & (sub + b0 <= hi)
            cs.append(jnp.where(ok, v1[a:a + 1] + v2[b0:b0 + SUBLANES], -jnp.inf))
            pays.append((i1[a:a + 1] * N_KEYS + i2[b0:b0 + SUBLANES]) * SLAB_ROWS)
            flats.append(jnp.where(ok, a * TOPK_HALF + sub + b0, big))
    vals, outs = [], []
    for _ in range(k):
        m = _all_sublanes(_tree(cs, jnp.maximum), jnp.maximum)
        pos = _all_sublanes(_tree([jnp.where(c == m, f, big) for c, f in zip(cs, flats)], jnp.minimum), jnp.minimum)
        sels = [f == pos for f in flats]
        pay = _all_sublanes(_tree([jnp.where(s, p, -1) for s, p in zip(sels, pays)], jnp.maximum), jnp.maximum)
        vals.append(m[0:1])
        outs.append(pay[0:1])
        cs = [jnp.where(s, -jnp.inf, c) for s, c in zip(sels, cs)]
    return jnp.concatenate(vals, axis=0), jnp.concatenate(outs, axis=0)


def _route_kernel(sct_ref, idx_ref, g_ref, idxt_ref):
    def head(hh, carry):
        v1, i1 = _top_keys(sct_ref.at[2 * hh], TOPK_HALF)
        v2, i2 = _top_keys(sct_ref.at[2 * hh + 1], TOPK_HALF)
        gv, gi = _top_pairs(v1, i1, v2, i2, TOPK)
        e = jnp.exp(gv - gv[0:1])
        rows = pl.ds(pl.multiple_of(hh * TOPK, TOPK), TOPK)
        g_ref[rows, :] = e / jnp.sum(e, axis=0, keepdims=True)
        idxt_ref[rows, :] = gi.astype(F32)
        return carry

    lax.fori_loop(0, HP, head, 0)
    idx_ref[...] = jnp.transpose(idxt_ref[...]).astype(jnp.int32)


def _routing(sct):
    _, _, t = sct.shape
    tl = ROUTE_TILE
    nsel = HP * TOPK
    return pl.pallas_call(
        _route_kernel,
        out_shape=(jax.ShapeDtypeStruct((t, nsel), jnp.int32), jax.ShapeDtypeStruct((nsel, t), F32)),
        grid=(t // tl,),
        in_specs=[pl.BlockSpec((2 * HP, N_KEYS, tl), lambda i: (0, 0, i))],
        out_specs=(pl.BlockSpec((tl, nsel), lambda i: (i, 0)), pl.BlockSpec((nsel, tl), lambda i: (0, i))),
        scratch_shapes=[pltpu.VMEM((nsel, tl), F32)],
        compiler_params=_cparams(("parallel",)),
        name="routing",
    )(sct)


SLAB_ROWS = 4
_HI_MASK = 0xFFFF0000


def _unpack(slab):
    lo = lax.bitcast_convert_type(slab << 16, F32)
    hi = lax.bitcast_convert_type(slab & jnp.uint32(_HI_MASK), F32)
    return lo, hi


def _gather_slabs(idx_ref, tab_ref, slab_ref, t):
    group = 8
    for j0 in range(0, idx_ref.shape[1], group):
        row = idx_ref.at[t, pl.ds(j0, group)]
        for k in range(group):
            j = j0 + k
            first = pl.multiple_of(row[k], SLAB_ROWS)
            slab_ref[SLAB_ROWS * j:SLAB_ROWS * (j + 1), :] = tab_ref[pl.ds(first, SLAB_ROWS), :]


def _two_phase_loop(tb, gather, math, init):
    gather(0, 0)
    gather(1, 1)

    def two_tokens(i, carry):
        t0 = 2 * i
        carry = math(0, t0, carry)
        carry = math(1, t0 + 1, carry)
        gather(0, jnp.minimum(t0 + 2, tb - 1))
        gather(1, jnp.minimum(t0 + 3, tb - 1))
        return carry

    return lax.fori_loop(0, tb // 2, two_tokens, init)


def _slab_rows(slab_ref, s, nsel):
    return slab_ref[pl.ds(s, nsel, stride=4), :]


def _expert_down_kernel(idx_ref, x_ref, g_ref, u_ref, w_ref, slab0_ref, slab1_ref):
    tb = x_ref.shape[0]
    nsel = idx_ref.shape[1]
    slabs = (slab0_ref, slab1_ref)
    lane = lax.broadcasted_iota(jnp.int32, (nsel, tb), 1)

    def gather(buf, t):
        _gather_slabs(idx_ref, u_ref, slabs[buf], t)

    def math(buf, t, at):
        xv = x_ref[t]
        acc = None
        for s in range(SLAB_ROWS):
            lo, hi = _unpack(_slab_rows(slabs[buf], s, nsel))
            term = lo * xv[s:s + 1] + hi * xv[SLAB_ROWS + s:SLAB_ROWS + s + 1]
            acc = term if acc is None else acc + term
        a = jnp.sum(acc, axis=1, keepdims=True)
        return jnp.where(lane == t, a, at)

    at = _two_phase_loop(tb, gather, math, jnp.zeros((nsel, tb), F32))
    w_ref[...] = jax.nn.gelu(at) * g_ref[...]


def _expert_down(idx, x3, g_t, u_tab):
    t = x3.shape[0]
    tb = EXPERT_TILE
    nsel = idx.shape[1]
    return pl.pallas_call(
        _expert_down_kernel,
        out_shape=jax.ShapeDtypeStruct((nsel, t), F32),
        grid=(t // tb,),
        in_specs=[
            pl.BlockSpec((tb, nsel), lambda i: (i, 0), memory_space=pltpu.SMEM),
            pl.BlockSpec((tb, 8, LANES), lambda i: (i, 0, 0)),
            pl.BlockSpec((nsel, tb), lambda i: (0, i)),
            _resident(u_tab.shape),
        ],
        out_specs=pl.BlockSpec((nsel, tb), lambda i: (0, i)),
        scratch_shapes=[pltpu.VMEM((4 * nsel, LANES), jnp.uint32), pltpu.VMEM((4 * nsel, LANES), jnp.uint32)],
        compiler_params=_cparams(("arbitrary",)),
        name="expert_down",
    )(idx, x3, g_t, u_tab)


def _expert_up_kernel(idx_ref, wt_ref, x1_ref, v_ref, gt2_ref, g2_ref, b2_ref, o_ref, acc_ref, y_ref,
                      slab0_ref, slab1_ref):
    tb = x1_ref.shape[0]
    nsel = idx_ref.shape[1]
    slabs = (slab0_ref, slab1_ref)

    def gather(buf, t):
        _gather_slabs(idx_ref, v_ref, slabs[buf], t)

    def math(buf, t, carry):
        wt = wt_ref[...]
        wb = jnp.take_along_axis(wt, jnp.full(wt.shape, t, jnp.int32), axis=1, mode="promise_in_bounds")
        los, his = [], []
        for s in range(SLAB_ROWS):
            lo, hi = _unpack(_slab_rows(slabs[buf], s, nsel))
            los.append(jnp.sum(lo * wb, axis=0, keepdims=True))
            his.append(jnp.sum(hi * wb, axis=0, keepdims=True))
        acc_ref[t] = jnp.concatenate(los + his, axis=0)
        return carry

    _two_phase_loop(tb, gather, math, 0)
    r = DEEPNORM_ALPHA * x1_ref[...] + gt2_ref[...] * acc_ref[...]
    n = r.shape[1] * r.shape[2]
    mu = jnp.sum(jnp.sum(r, axis=2, keepdims=True), axis=1, keepdims=True) / n
    c = r - mu
    var = jnp.sum(jnp.sum(c * c, axis=2, keepdims=True), axis=1, keepdims=True) / n
    y_ref[...] = (c * lax.rsqrt(var + EPS) * g2_ref[...] + b2_ref[...]).reshape(tb * 8, LANES)
    for ch in range(8):
        o_ref[:, ch * LANES:(ch + 1) * LANES] = y_ref[pl.ds(ch, tb, stride=8), :]


def _expert_up(idx, w_t, x13, v_tab, gt2, ln2_g, ln2_b, tiles_per_batch):
    t = x13.shape[0]
    tb = EXPERT_TILE
    nsel = idx.shape[1]
    tok = pl.BlockSpec((tb, 8, LANES), lambda i: (i, 0, 0))
    return pl.pallas_call(
        _expert_up_kernel,
        out_shape=jax.ShapeDtypeStruct((t, 8 * LANES), F32),
        grid=(t // tb,),
        in_specs=[pl.BlockSpec((tb, nsel), lambda i: (i, 0), memory_space=pltpu.SMEM),
                  pl.BlockSpec((nsel, tb), lambda i: (0, i)), tok, _resident(v_tab.shape),
                  pl.BlockSpec((1, 8, LANES), lambda i: (i // tiles_per_batch, 0, 0)),
                  _full((1, 8, LANES)), _full((1, 8, LANES))],
        out_specs=pl.BlockSpec((tb, 8 * LANES), lambda i: (i, 0)),
        scratch_shapes=[pltpu.VMEM((tb, 8, LANES), F32), pltpu.VMEM((tb * 8, LANES), F32),
                        pltpu.VMEM((4 * nsel, LANES), jnp.uint32), pltpu.VMEM((4 * nsel, LANES), jnp.uint32)],
        compiler_params=_cparams(("arbitrary",)),
        name="expert_up",
    )(idx, w_t, x13, v_tab, gt2, ln2_g, ln2_b)


def _pack_table(tab):
    e, d = tab.shape
    bits = lax.bitcast_convert_type(tab.astype(BF16), jnp.uint16).astype(jnp.uint32)
    half = d // 2
    word = bits[:, :half] | (bits[:, half:] << 16)
    assert half == SLAB_ROWS * LANES
    return word.reshape(e * SLAB_ROWS, LANES)


def _rope_tables(n, nctx):
    rows = n // GRID_W
    quarter = DA // 4
    inv = ROPE_BASE ** (-jnp.arange(quarter, dtype=F32) / quarter)
    row_ang = jnp.arange(rows, dtype=F32)[:, None] * inv
    col_ang = jnp.arange(GRID_W, dtype=F32)[:, None] * inv
    ang = jnp.concatenate([
        jnp.broadcast_to(row_ang[:, None, :], (rows, GRID_W, quarter)),
        jnp.broadcast_to(col_ang[None, :, :], (rows, GRID_W, quarter))], -1).reshape(n, 2 * quarter)
    cos = jnp.concatenate([jnp.ones((nctx, 2 * quarter), F32), jnp.cos(ang)], 0)
    sin = jnp.concatenate([jnp.zeros((nctx, 2 * quarter), F32), jnp.sin(ang)], 0)
    one, zero = jnp.ones_like(cos), jnp.zeros_like(sin)
    cf = jnp.concatenate([cos, cos, cos, cos], 1)
    sf = jnp.concatenate([-sin, sin, -sin, sin], 1)
    ch = jnp.concatenate([cos, cos, one, one], 1)
    sh = jnp.concatenate([-sin, sin, zero, zero], 1)
    return cf, sf, ch, sh


def kernel(x, c, ctx, c_ctx, w_mod, b_mod, w_in, w_q_up, q_norm_g, w_kv_up, kv_norm_g, lambda_q1, lambda_k1,
           lambda_q2, lambda_k2, subln_g, w_pa, w_pb, w_out, ln1_g, ln1_b, w_pq, peer_keys, peer_u, peer_v,
           ln2_g, ln2_b):
    bsz, n, d = x.shape
    nctx = ctx.shape[1]
    t = bsz * n
    assert d == 8 * LANES and n % TOKEN_TILE == 0 and nctx % TOKEN_TILE == 0 and n % GRID_W == 0
    assert t % ROUTE_TILE == 0 and n % EXPERT_TILE == 0 and w_mod.shape[0] == DEPTH

    rows = -(-(bsz + 1) // 8) * 8
    c_rows = jnp.concatenate([c, c_ctx[None, :], jnp.zeros((rows - bsz - 1, d), F32)], 0)
    mod = _modulation(c_rows, w_mod[0], b_mod[0][None, :])[:bsz + 1]
    sh1, sc1, gt1, sh2, sc2, gt2 = [m[:, None, :] for m in jnp.split(mod, 6, axis=-1)]

    w_all = jnp.concatenate([w_in[0], jnp.zeros((d, LANES - ROPE), F32)], 1).astype(BF16)
    wq = w_q_up[0].reshape(Q_LORA, HB, NOPE + ROPE)
    wq = jnp.concatenate([wq, jnp.zeros((Q_LORA, HB, 256 - NOPE - ROPE), F32)], -1).reshape(Q_LORA, HB * 256)
    cf, sf, ch, shh = _rope_tables(n, nctx)
    lam_rows = jnp.zeros((8, LANES), F32).at[0:4, 0:DA].set(
        jnp.stack([lambda_q1[0], lambda_k1[0], lambda_q2[0], lambda_k2[0]]))

    qa, qm, gs, ka, va, km, vb = _projection(
        x, ctx, sc1, sh1, w_all, q_norm_g[0][None, :], kv_norm_g[0][None, :], wq.astype(BF16),
        w_kv_up[0].astype(BF16), cf, sf, ch, shh)

    ya, yb = _mixers(qa, ka, va, qm, km, vb, lam_rows, subln_g[0][None, :])

    keys = peer_keys[0].reshape(2 * HP, N_KEYS, DK_HALF).astype(BF16)
    x1, h2, sct = _merge(
        ya.reshape(t, d), yb.reshape(t, d), gs.reshape(t, 2 * d), x.reshape(t, d),
        w_pa[0].astype(BF16), w_pb[0].astype(BF16), w_out[0].astype(BF16),
        gt1[:bsz], sc2[:bsz], sh2[:bsz], ln1_g[0][None, :], ln1_b[0][None, :],
        w_pq[0].astype(BF16), keys, n // TOKEN_TILE)

    idx, g_t = _routing(sct)

    w_t = _expert_down(idx, h2.reshape(t, 8, LANES), g_t, _pack_table(peer_u[0]))
    out = _expert_up(idx, w_t, x1.reshape(t, 8, LANES), _pack_table(peer_v[0]),
                     gt2[:bsz].reshape(bsz, 8, LANES), ln2_g[0].reshape(1, 8, LANES),
                     ln2_b[0].reshape(1, 8, LANES), n // EXPERT_TILE)
    return out.reshape(bsz, n, d)
```

```python
import functools
import math

import jax
import jax.numpy as jnp
from jax import lax
from jax.experimental import pallas as pl
from jax.experimental.pallas import tpu as pltpu

GRID_W = 64
HA, DA = 8, 64
HB, NOPE, ROPE, VB = 8, 128, 64, 128
Q_LORA, KV_LORA = 256, 256
N_KEYS = 128
HP, DK_HALF = 8, 128
TOPK_HALF, TOPK = 16, 16
ROPE_BASE = 10000.0
EPS = 1e-6
DEPTH = 1
DEEPNORM_ALPHA = (2 * DEPTH) ** 0.25
LAMBDA_INIT = 0.8 - 0.6 * math.exp(-0.3 * 0)
LOG2E = 1.0 / math.log(2.0)
QA_SCALE = DA ** -0.5 * LOG2E
QM_SCALE = (NOPE + ROPE) ** -0.5 * LOG2E

LANES = 128
VMEM_LIMIT_BYTES = 56 * 1024 * 1024

TOKEN_TILE = 256
Q_TILE = 1024
EXPERT_TILE = 128

F32 = jnp.float32
BF16 = jnp.bfloat16


def _cparams(sem):
    return pltpu.CompilerParams(dimension_semantics=sem, vmem_limit_bytes=VMEM_LIMIT_BYTES)


def _full(shape):
    n = len(shape)
    return pl.BlockSpec(shape, lambda *_: (0,) * n)


def _resident(shape):
    n = len(shape)
    return pl.BlockSpec(shape, lambda *_: (0,) * n, pipeline_mode=pl.Buffered(1))


def _mod_kernel(c_ref, w_ref, b_ref, o_ref):
    c = c_ref[...]
    s = c * jax.nn.sigmoid(c)
    o_ref[...] = jnp.dot(s, w_ref[...], preferred_element_type=F32) + b_ref[...]


def _modulation(c_rows, w_mod, b_mod):
    r, d = c_rows.shape
    n = w_mod.shape[1]
    tn = 1536
    return pl.pallas_call(
        _mod_kernel,
        out_shape=jax.ShapeDtypeStruct((r, n), F32),
        grid=(n // tn,),
        in_specs=[_full((r, d)), pl.BlockSpec((d, tn), lambda j: (0, j)), pl.BlockSpec((1, tn), lambda j: (0, j))],
        out_specs=pl.BlockSpec((r, tn), lambda j: (0, j)),
        compiler_params=_cparams(("arbitrary",)),
        name="modulation",
    )(c_rows, w_mod, b_mod)


def _swap_halves(blk):
    lane = lax.broadcasted_iota(jnp.int32, blk.shape, 1)
    up = pltpu.roll(blk, LANES - 32, axis=1)
    dn = pltpu.roll(blk, 32, axis=1)
    return jnp.where((lane % 64) < 32, up, dn)


def _rope_block(blk, c, s):
    return blk * c + _swap_halves(blk) * s


def _rms(x, g):
    return x * lax.rsqrt(jnp.mean(x * x, axis=-1, keepdims=True) + EPS) * g


def _proj_kernel(nct, x_ref, ctx_ref, sc_ref, sh_ref, w_ref, qg_ref, kvg_ref, wq_ref, wkv_ref,
                 cf_ref, sf_ref, ch_ref, shh_ref,
                 qa_ref, qm_ref, gs_ref, ka_ref, va_ref, km_ref, vb_ref):
    i = pl.program_id(1)
    d = x_ref.shape[-1]
    xt = jnp.where(i < nct, ctx_ref[0], x_ref[0])
    h = xt * (1.0 + sc_ref[0]) + sh_ref[0]
    hb = h.astype(BF16)
    cf, sf = cf_ref[...], sf_ref[...]
    ch, shh = ch_ref[...], shh_ref[...]

    def proj(lo, hi):
        return jnp.dot(hb, w_ref[:, lo:hi], preferred_element_type=F32)

    c_qa, c_ql, c_g = 0, HA * 2 * DA, HA * 2 * DA + Q_LORA
    c_ka = c_g + 2 * d
    c_va = c_ka + HA * 2 * DA
    c_kvl = c_va + HA * 2 * DA
    c_kr = c_kvl + KV_LORA

    ka = proj(c_ka, c_va)
    for b in range(ka.shape[1] // LANES):
        sl = slice(b * LANES, (b + 1) * LANES)
        ka_ref[0, :, sl] = _rope_block(ka[:, sl], cf, sf).astype(BF16)
    lane = lax.broadcasted_iota(jnp.int32, (hb.shape[0], LANES), 1)
    ones_col = jnp.where(lane == 0, 1.0, 0.0).astype(BF16)
    va = proj(c_va, c_kvl).astype(BF16)
    for hh in range(HA):
        va_ref[0, :, hh * 256:hh * 256 + 2 * DA] = va[:, hh * 2 * DA:(hh + 1) * 2 * DA]
        va_ref[0, :, hh * 256 + 2 * DA:(hh + 1) * 256] = ones_col
    kvl = _rms(proj(c_kvl, c_kr), kvg_ref[...]).astype(BF16)
    kv = jnp.dot(kvl, wkv_ref[...], preferred_element_type=F32)
    kr = _rope_block(proj(c_kr, c_kr + LANES), ch, shh).astype(BF16)
    for hh in range(HB):
        base = hh * (NOPE + VB)
        km_ref[0, :, hh * 256:hh * 256 + NOPE] = kv[:, base:base + NOPE].astype(BF16)
        km_ref[0, :, hh * 256 + NOPE:(hh + 1) * 256] = kr
        vb_ref[0, :, hh * 256:hh * 256 + VB] = kv[:, base + NOPE:base + NOPE + VB].astype(BF16)
        vb_ref[0, :, hh * 256 + VB:(hh + 1) * 256] = ones_col

    @pl.when(i >= nct)
    def _():
        qa = proj(c_qa, c_ql)
        for b in range(qa.shape[1] // LANES):
            sl = slice(b * LANES, (b + 1) * LANES)
            qa_ref[0, :, sl] = (_rope_block(qa[:, sl], cf, sf) * QA_SCALE).astype(BF16)
        ql = _rms(proj(c_ql, c_g), qg_ref[...]).astype(BF16)
        qb = jnp.dot(ql, wq_ref[...], preferred_element_type=F32) * QM_SCALE
        for hh in range(HB):
            qm_ref[0, :, hh * 256:hh * 256 + NOPE] = qb[:, hh * 256:hh * 256 + NOPE].astype(BF16)
            qm_ref[0, :, hh * 256 + NOPE:(hh + 1) * 256] = _rope_block(
                qb[:, hh * 256 + NOPE:(hh + 1) * 256], ch, shh).astype(BF16)
        gs_ref[0] = jax.nn.sigmoid(proj(c_g, c_ka)).astype(BF16)


def _projection(x, ctx, sc1, sh1, w_all, q_norm_g, kv_norm_g, wq, wkv, cf, sf, ch, shh):
    bsz, n, d = x.shape
    nctx = ctx.shape[1]
    tn = TOKEN_TILE
    nct = nctx // tn
    s = nctx + n
    ncols = w_all.shape[1]

    def lat(b, i):
        return (b, jnp.maximum(i - nct, 0), 0)

    def cx(b, i):
        return (b, jnp.minimum(i, nct - 1), 0)

    def modrow(b, i):
        return (jnp.where(i < nct, bsz, b), 0, 0)

    def allk(b, i):
        return (b, i, 0)

    def tab(b, i):
        return (i, 0)

    bf = lambda w: jax.ShapeDtypeStruct((bsz, n, w), BF16)
    bfs = lambda w: jax.ShapeDtypeStruct((bsz, s, w), BF16)
    return pl.pallas_call(
        functools.partial(_proj_kernel, nct),
        out_shape=(bf(HA * 2 * DA), bf(HB * 256), bf(2 * d), bfs(HA * 2 * DA), bfs(HA * 256), bfs(HB * 256), bfs(HB * 256)),
        grid=(bsz, s // tn),
        in_specs=[
            pl.BlockSpec((1, tn, d), lat), pl.BlockSpec((1, tn, d), cx),
            pl.BlockSpec((1, 1, d), modrow), pl.BlockSpec((1, 1, d), modrow),
            _resident((d, ncols)), _full((1, Q_LORA)), _full((1, KV_LORA)),
            _resident(wq.shape), _resident(wkv.shape),
            pl.BlockSpec((tn, LANES), tab), pl.BlockSpec((tn, LANES), tab),
            pl.BlockSpec((tn, LANES), tab), pl.BlockSpec((tn, LANES), tab),
        ],
        out_specs=(
            pl.BlockSpec((1, tn, HA * 2 * DA), lat), pl.BlockSpec((1, tn, HB * 256), lat),
            pl.BlockSpec((1, tn, 2 * d), lat),
            pl.BlockSpec((1, tn, HA * 2 * DA), allk), pl.BlockSpec((1, tn, HA * 256), allk),
            pl.BlockSpec((1, tn, HB * 256), allk), pl.BlockSpec((1, tn, HB * 256), allk),
        ),
        compiler_params=_cparams(("parallel", "arbitrary")),
        name="projection",
    )(x, ctx, sc1, sh1, w_all, q_norm_g, kv_norm_g, wq, wkv, cf, sf, ch, shh)


_NT = (((1,), (1,)), ((), ()))


def _softmax_pv(q, k, v_ones):
    s = lax.dot_general(q, k, _NT, preferred_element_type=F32)
    m = jnp.max(s, axis=-1, keepdims=True)
    p = jnp.exp2(s - m).astype(BF16)
    o = jnp.dot(p, v_ones, preferred_element_type=F32)
    w = o.shape[1] // 2
    return o[:, :w] / o[:, w:w + 1]


def _mixers_kernel(qa_ref, ka_ref, va_ref, qm_ref, km_ref, vb_ref, lam_ref, g_ref, ya_ref, yb_ref):
    q = qa_ref[0]
    k = ka_ref[0]
    v = va_ref[0]
    lane = lax.broadcasted_iota(jnp.int32, q.shape, 1)
    zero = jnp.zeros_like(q)
    o1 = _softmax_pv(jnp.where(lane < DA, q, zero), k, v)
    o2 = _softmax_pv(jnp.where(lane < DA, zero, q), k, v)
    lv = lam_ref[...]
    lam = (jnp.exp(jnp.sum(lv[0:1] * lv[1:2], axis=-1, keepdims=True))
           - jnp.exp(jnp.sum(lv[2:3] * lv[3:4], axis=-1, keepdims=True)) + LAMBDA_INIT)
    o = o1 - lam * o2
    y = o * lax.rsqrt(jnp.mean(o * o, axis=-1, keepdims=True) + EPS) * g_ref[...]
    ya_ref[0] = (y * (1.0 - LAMBDA_INIT)).astype(BF16)
    yb_ref[0] = _softmax_pv(qm_ref[0], km_ref[0], vb_ref[0]).astype(BF16)


def _mixers(qa, ka, va, qm, km, vb, lam_rows, subln_g):
    assert HA == HB
    bsz, n, w = qa.shape
    s = ka.shape[1]
    tq = Q_TILE
    qspec = lambda width: pl.BlockSpec((1, tq, width), lambda b, h, i: (b, i, h))
    kspec = lambda width: pl.BlockSpec((1, s, width), lambda b, h, i: (b, 0, h))
    return pl.pallas_call(
        _mixers_kernel,
        out_shape=(jax.ShapeDtypeStruct((bsz, n, w), BF16), jax.ShapeDtypeStruct((bsz, n, HB * VB), BF16)),
        grid=(bsz, HA, n // tq),
        in_specs=[qspec(LANES), kspec(LANES), kspec(256), qspec(256), kspec(256), kspec(256),
                  _full(lam_rows.shape), _full(subln_g.shape)],
        out_specs=(qspec(LANES), qspec(VB)),
        compiler_params=_cparams(("parallel", "parallel", "arbitrary")),
        name="token_mixers",
    )(qa, ka, va, qm, km, vb, lam_rows, subln_g)


def _layer_norm(r, g, b):
    mu = jnp.mean(r, axis=-1, keepdims=True)
    c = r - mu
    var = jnp.mean(c * c, axis=-1, keepdims=True)
    return c * lax.rsqrt(var + EPS) * g + b


def _merge_kernel(ya_ref, yb_ref, gs_ref, x_ref, wpa_ref, wpb_ref, wo_ref, gt1_ref, sc2_ref, sh2_ref,
                  g1_ref, b1_ref, wpq_ref, keys_ref, x1_ref, h2_ref, idx_ref, g_ref, sct_ref, idxt_ref):
    d = x_ref.shape[-1]
    gs = gs_ref[...].astype(F32)
    za = jnp.dot(ya_ref[...], wpa_ref[...], preferred_element_type=F32)
    zb = jnp.dot(yb_ref[...], wpb_ref[...], preferred_element_type=F32)
    z = gs[:, :d] * za + gs[:, d:] * zb
    y = jnp.dot(z.astype(BF16), wo_ref[...], preferred_element_type=F32)
    x1 = _layer_norm(DEEPNORM_ALPHA * x_ref[...] + gt1_ref[0] * y, g1_ref[...], b1_ref[...])
    h2 = x1 * (1.0 + sc2_ref[0]) + sh2_ref[0]
    tn = x1.shape[0]
    for ch in range(d // LANES):
        x1_ref[pl.ds(ch, tn, stride=8), :] = x1[:, ch * LANES:(ch + 1) * LANES]
        h2_ref[pl.ds(ch, tn, stride=8), :] = h2[:, ch * LANES:(ch + 1) * LANES]
    qp = jnp.dot(h2.astype(BF16), wpq_ref[...], preferred_element_type=F32).astype(BF16)
    for hh in range(2 * HP):
        sct_ref[hh] = lax.dot_general(keys_ref[hh], qp[:, hh * DK_HALF:(hh + 1) * DK_HALF], _NT,
                                      preferred_element_type=F32)
    _route(sct_ref, idx_ref, g_ref, idxt_ref)


def _merge(ya, yb, gs, x2d, wpa, wpb, wo, gt1, sc2, sh2, ln1_g, ln1_b, wpq, keys, tiles_per_batch):
    t, d = x2d.shape
    tn = TOKEN_TILE
    row = lambda i: (i, 0)
    brow = lambda i: (i // tiles_per_batch, 0, 0)
    return pl.pallas_call(
        _merge_kernel,
        out_shape=(jax.ShapeDtypeStruct((t * 8, LANES), F32), jax.ShapeDtypeStruct((t * 8, LANES), F32),
                   jax.ShapeDtypeStruct((t, HP * TOPK), jnp.int32), jax.ShapeDtypeStruct((HP * TOPK, t), F32)),
        grid=(t // tn,),
        in_specs=[
            pl.BlockSpec((tn, d), row), pl.BlockSpec((tn, d), row), pl.BlockSpec((tn, 2 * d), row),
            pl.BlockSpec((tn, d), row),
            _resident(wpa.shape), _resident(wpb.shape), _resident(wo.shape),
            pl.BlockSpec((1, 1, d), brow), pl.BlockSpec((1, 1, d), brow), pl.BlockSpec((1, 1, d), brow),
            _full((1, d)), _full((1, d)), _resident(wpq.shape), _resident(keys.shape),
        ],
        out_specs=(pl.BlockSpec((tn * 8, LANES), row), pl.BlockSpec((tn * 8, LANES), row),
                   pl.BlockSpec((tn, HP * TOPK), row), pl.BlockSpec((HP * TOPK, tn), lambda i: (0, i))),
        scratch_shapes=[pltpu.VMEM((2 * HP, N_KEYS, tn), F32), pltpu.VMEM((HP * TOPK, tn), F32)],
        compiler_params=_cparams(("parallel",)),
        name="merge",
    )(ya, yb, gs, x2d, wpa, wpb, wo, gt1, sc2, sh2, ln1_g, ln1_b, wpq, keys)


SUBLANES = 8


def _tree(items, merge):
    while len(items) > 1:
        nxt = [merge(items[k], items[k + 1]) for k in range(0, len(items) - 1, 2)]
        if len(items) % 2:
            nxt.append(items[-1])
        items = nxt
    return items[0]


def _all_sublanes(x, op):
    for sh in (4, 2, 1):
        x = op(x, pltpu.roll(x, sh, axis=0))
    return x


def _top_keys(s_ref, k):
    nblk = s_ref.shape[0] // SUBLANES
    tl = s_ref.shape[1]
    sub = lax.broadcasted_iota(jnp.int32, (SUBLANES, tl), 0)
    rows = [sub + SUBLANES * b for b in range(nblk)]
    vs = [s_ref[SUBLANES * b:SUBLANES * (b + 1), :] for b in range(nblk)]

    def merge(a, b):
        c = a[0] >= b[0]
        return jnp.where(c, a[0], b[0]), jnp.where(c, a[1], b[1])

    vals, idxs = [], []
    for _ in range(k):
        v, i = _tree(list(zip(vs, rows)), merge)
        for sh in (4, 2, 1):
            pv, pi = pltpu.roll(v, sh, axis=0), pltpu.roll(i, sh, axis=0)
            c = (v > pv) | ((v == pv) & (i < pi))
            v, i = jnp.where(c, v, pv), jnp.where(c, i, pi)
        vals.append(v[0:1])
        idxs.append(i[0:1])
        vs = [jnp.where(rows[b] == i, -jnp.inf, vs[b]) for b in range(nblk)]
    return jnp.concatenate(vals, axis=0), jnp.concatenate(idxs, axis=0)


_CAND_BLOCKS = (("col", 0, 0), ("col", 0, 8), ("col", 1, 0),
                ("row", 0, 0, 2, 7), ("row", 0, 8, 8, 15), ("row", 1, 0, 2, 7),
                ("row", 2, 0, 2, 4), ("row", 3, 0, 2, 3), ("row", 4, 0, 2, 2))
assert TOPK == 16 and TOPK_HALF == 16 and sum(
    SUBLANES if blk[0] == "col" else blk[4] - blk[3] + 1 for blk in _CAND_BLOCKS) == sum(
    1 for a in range(16) for b in range(16) if (a + 1) * (b + 1) <= TOPK)


def _top_pairs(v1, i1, v2, i2, k):
    tl = v1.shape[1]
    sub = lax.broadcasted_iota(jnp.int32, (SUBLANES, tl), 0)
    big = TOPK_HALF * TOPK_HALF
    cs, flats, pays = [], [], []
    for blk in _CAND_BLOCKS:
        if blk[0] == "col":
            _, b, a0 = blk
            cs.append(v1[a0:a0 + SUBLANES] + v2[b:b + 1])
            pays.append((i1[a0:a0 + SUBLANES] * N_KEYS + i2[b:b + 1]) * SLAB_ROWS)
            flats.append((sub + a0) * TOPK_HALF + b)
        else:
            _, a, b0, lo, hi = blk
            ok = (sub + b0 >= lo) & (sub + b0 <= hi)
            cs.append(jnp.where(ok, v1[a:a + 1] + v2[b0:b0 + SUBLANES], -jnp.inf))
            pays.append((i1[a:a + 1] * N_KEYS + i2[b0:b0 + SUBLANES]) * SLAB_ROWS)
            flats.append(jnp.where(ok, a * TOPK_HALF + sub + b0, big))
    vals, outs = [], []
    for _ in range(k):
        m = _all_sublanes(_tree(cs, jnp.maximum), jnp.maximum)
        pos = _all_sublanes(_tree([jnp.where(c == m, f, big) for c, f in zip(cs, flats)], jnp.minimum), jnp.minimum)
        sels = [f == pos for f in flats]
        pay = _all_sublanes(_tree([jnp.where(s, p, -1) for s, p in zip(sels, pays)], jnp.maximum), jnp.maximum)
        vals.append(m[0:1])
        outs.append(pay[0:1])
        cs = [jnp.where(s, -jnp.inf, c) for s, c in zip(sels, cs)]
    return jnp.concatenate(vals, axis=0), jnp.concatenate(outs, axis=0)


def _route(sct_ref, idx_ref, g_ref, idxt_ref):
    def head(hh):
        v1, i1 = _top_keys(sct_ref.at[2 * hh], TOPK_HALF)
        v2, i2 = _top_keys(sct_ref.at[2 * hh + 1], TOPK_HALF)
        gv, gi = _top_pairs(v1, i1, v2, i2, TOPK)
        e = jnp.exp(gv - gv[0:1])
        rows = pl.ds(pl.multiple_of(hh * TOPK, TOPK), TOPK)
        g_ref[rows, :] = e / jnp.sum(e, axis=0, keepdims=True)
        idxt_ref[rows, :] = gi.astype(F32)

    def two_heads(hp, carry):
        head(2 * hp)
        head(2 * hp + 1)
        return carry

    lax.fori_loop(0, HP // 2, two_heads, 0)
    idx_ref[...] = jnp.transpose(idxt_ref[...]).astype(jnp.int32)


SLAB_ROWS = 4
_HI_MASK = 0xFFFF0000


def _unpack(slab):
    lo = lax.bitcast_convert_type(slab << 16, F32)
    hi = lax.bitcast_convert_type(slab & jnp.uint32(_HI_MASK), F32)
    return lo, hi


def _gather_slabs(idx_ref, tab_ref, slab_ref, t):
    group = 8
    for j0 in range(0, idx_ref.shape[1], group):
        row = idx_ref.at[t, pl.ds(j0, group)]
        for k in range(group):
            j = j0 + k
            first = pl.multiple_of(row[k], SLAB_ROWS)
            slab_ref[SLAB_ROWS * j:SLAB_ROWS * (j + 1), :] = tab_ref[pl.ds(first, SLAB_ROWS), :]


def _two_phase_loop(tb, gather, math, init):
    gather(0, 0)
    gather(1, 1)

    def two_tokens(i, carry):
        t0 = 2 * i
        carry = math(0, t0, carry)
        carry = math(1, t0 + 1, carry)
        gather(0, jnp.minimum(t0 + 2, tb - 1))
        gather(1, jnp.minimum(t0 + 3, tb - 1))
        return carry

    return lax.fori_loop(0, tb // 2, two_tokens, init)


def _slab_rows(slab_ref, s, nsel):
    return slab_ref[pl.ds(s, nsel, stride=4), :]


def _expert_down_kernel(idx_ref, x_ref, g_ref, u_ref, w_ref, slab0_ref, slab1_ref):
    tb = x_ref.shape[0]
    nsel = idx_ref.shape[1]
    slabs = (slab0_ref, slab1_ref)
    lane = lax.broadcasted_iota(jnp.int32, (nsel, tb), 1)

    def gather(buf, t):
        _gather_slabs(idx_ref, u_ref, slabs[buf], t)

    def math(buf, t, at):
        xv = x_ref[t]
        acc = None
        for s in range(SLAB_ROWS):
            lo, hi = _unpack(_slab_rows(slabs[buf], s, nsel))
            term = lo * xv[s:s + 1] + hi * xv[SLAB_ROWS + s:SLAB_ROWS + s + 1]
            acc = term if acc is None else acc + term
        a = jnp.sum(acc, axis=1, keepdims=True)
        return jnp.where(lane == t, a, at)

    at = _two_phase_loop(tb, gather, math, jnp.zeros((nsel, tb), F32))
    w_ref[...] = jax.nn.gelu(at) * g_ref[...]


def _expert_down(idx, x3, g_t, u_tab):
    t = x3.shape[0]
    tb = EXPERT_TILE
    nsel = idx.shape[1]
    return pl.pallas_call(
        _expert_down_kernel,
        out_shape=jax.ShapeDtypeStruct((nsel, t), F32),
        grid=(t // tb,),
        in_specs=[
            pl.BlockSpec((tb, nsel), lambda i: (i, 0), memory_space=pltpu.SMEM),
            pl.BlockSpec((tb, 8, LANES), lambda i: (i, 0, 0)),
            pl.BlockSpec((nsel, tb), lambda i: (0, i)),
            _resident(u_tab.shape),
        ],
        out_specs=pl.BlockSpec((nsel, tb), lambda i: (0, i)),
        scratch_shapes=[pltpu.VMEM((4 * nsel, LANES), jnp.uint32), pltpu.VMEM((4 * nsel, LANES), jnp.uint32)],
        compiler_params=_cparams(("arbitrary",)),
        name="expert_down",
    )(idx, x3, g_t, u_tab)


def _expert_up_kernel(idx_ref, wt_ref, x1_ref, v_ref, gt2_ref, g2_ref, b2_ref, o_ref, acc_ref, y_ref,
                      slab0_ref, slab1_ref):
    tb = x1_ref.shape[0]
    nsel = idx_ref.shape[1]
    slabs = (slab0_ref, slab1_ref)

    def gather(buf, t):
        _gather_slabs(idx_ref, v_ref, slabs[buf], t)

    def math(buf, t, carry):
        wt = wt_ref[...]
        wb = jnp.take_along_axis(wt, jnp.full(wt.shape, t, jnp.int32), axis=1, mode="promise_in_bounds")
        los, his = [], []
        for s in range(SLAB_ROWS):
            lo, hi = _unpack(_slab_rows(slabs[buf], s, nsel))
            los.append(jnp.sum(lo * wb, axis=0, keepdims=True))
            his.append(jnp.sum(hi * wb, axis=0, keepdims=True))
        acc_ref[t] = jnp.concatenate(los + his, axis=0)
        return carry

    _two_phase_loop(tb, gather, math, 0)
    r = DEEPNORM_ALPHA * x1_ref[...] + gt2_ref[...] * acc_ref[...]
    n = r.shape[1] * r.shape[2]
    mu = jnp.sum(jnp.sum(r, axis=2, keepdims=True), axis=1, keepdims=True) / n
    c = r - mu
    var = jnp.sum(jnp.sum(c * c, axis=2, keepdims=True), axis=1, keepdims=True) / n
    y_ref[...] = (c * lax.rsqrt(var + EPS) * g2_ref[...] + b2_ref[...]).reshape(tb * 8, LANES)
    for ch in range(8):
        o_ref[:, ch * LANES:(ch + 1) * LANES] = y_ref[pl.ds(ch, tb, stride=8), :]


def _expert_up(idx, w_t, x13, v_tab, gt2, ln2_g, ln2_b, tiles_per_batch):
    t = x13.shape[0]
    tb = EXPERT_TILE
    nsel = idx.shape[1]
    tok = pl.BlockSpec((tb, 8, LANES), lambda i: (i, 0, 0))
    return pl.pallas_call(
        _expert_up_kernel,
        out_shape=jax.ShapeDtypeStruct((t, 8 * LANES), F32),
        grid=(t // tb,),
        in_specs=[pl.BlockSpec((tb, nsel), lambda i: (i, 0), memory_space=pltpu.SMEM),
                  pl.BlockSpec((nsel, tb), lambda i: (0, i)), tok, _resident(v_tab.shape),
                  pl.BlockSpec((1, 8, LANES), lambda i: (i // tiles_per_batch, 0, 0)),
                  _full((1, 8, LANES)), _full((1, 8, LANES))],
        out_specs=pl.BlockSpec((tb, 8 * LANES), lambda i: (i, 0)),
        scratch_shapes=[pltpu.VMEM((tb, 8, LANES), F32), pltpu.VMEM((tb * 8, LANES), F32),
                        pltpu.VMEM((4 * nsel, LANES), jnp.uint32), pltpu.VMEM((4 * nsel, LANES), jnp.uint32)],
        compiler_params=_cparams(("arbitrary",)),
        name="expert_up",
    )(idx, w_t, x13, v_tab, gt2, ln2_g, ln2_b)


def _pack_table(tab):
    e, d = tab.shape
    bits = lax.bitcast_convert_type(tab.astype(BF16), jnp.uint16).astype(jnp.uint32)
    half = d // 2
    word = bits[:, :half] | (bits[:, half:] << 16)
    assert half == SLAB_ROWS * LANES
    return word.reshape(e * SLAB_ROWS, LANES)


def _rope_tables(n, nctx):
    rows = n // GRID_W
    quarter = DA // 4
    inv = ROPE_BASE ** (-jnp.arange(quarter, dtype=F32) / quarter)
    row_ang = jnp.arange(rows, dtype=F32)[:, None] * inv
    col_ang = jnp.arange(GRID_W, dtype=F32)[:, None] * inv
    ang = jnp.concatenate([
        jnp.broadcast_to(row_ang[:, None, :], (rows, GRID_W, quarter)),
        jnp.broadcast_to(col_ang[None, :, :], (rows, GRID_W, quarter))], -1).reshape(n, 2 * quarter)
    cos = jnp.concatenate([jnp.ones((nctx, 2 * quarter), F32), jnp.cos(ang)], 0)
    sin = jnp.concatenate([jnp.zeros((nctx, 2 * quarter), F32), jnp.sin(ang)], 0)
    one, zero = jnp.ones_like(cos), jnp.zeros_like(sin)
    cf = jnp.concatenate([cos, cos, cos, cos], 1)
    sf = jnp.concatenate([-sin, sin, -sin, sin], 1)
    ch = jnp.concatenate([cos, cos, one, one], 1)
    sh = jnp.concatenate([-sin, sin, zero, zero], 1)
    return cf, sf, ch, sh


def kernel(x, c, ctx, c_ctx, w_mod, b_mod, w_in, w_q_up, q_norm_g, w_kv_up, kv_norm_g, lambda_q1, lambda_k1,
           lambda_q2, lambda_k2, subln_g, w_pa, w_pb, w_out, ln1_g, ln1_b, w_pq, peer_keys, peer_u, peer_v,
           ln2_g, ln2_b):
    bsz, n, d = x.shape
    nctx = ctx.shape[1]
    t = bsz * n
    assert d == 8 * LANES and n % TOKEN_TILE == 0 and nctx % TOKEN_TILE == 0 and n % GRID_W == 0
    assert n % EXPERT_TILE == 0 and w_mod.shape[0] == DEPTH

    rows = -(-(bsz + 1) // 8) * 8
    c_rows = jnp.concatenate([c, c_ctx[None, :], jnp.zeros((rows - bsz - 1, d), F32)], 0)
    mod = _modulation(c_rows, w_mod[0], b_mod[0][None, :])[:bsz + 1]
    sh1, sc1, gt1, sh2, sc2, gt2 = [m[:, None, :] for m in jnp.split(mod, 6, axis=-1)]

    w_all = jnp.concatenate([w_in[0], jnp.zeros((d, LANES - ROPE), F32)], 1).astype(BF16)
    wq = w_q_up[0].reshape(Q_LORA, HB, NOPE + ROPE)
    wq = jnp.concatenate([wq, jnp.zeros((Q_LORA, HB, 256 - NOPE - ROPE), F32)], -1).reshape(Q_LORA, HB * 256)
    cf, sf, ch, shh = _rope_tables(n, nctx)
    lam_rows = jnp.zeros((8, LANES), F32).at[0:4, 0:DA].set(
        jnp.stack([lambda_q1[0], lambda_k1[0], lambda_q2[0], lambda_k2[0]]))

    qa, qm, gs, ka, va, km, vb = _projection(
        x, ctx, sc1, sh1, w_all, q_norm_g[0][None, :], kv_norm_g[0][None, :], wq.astype(BF16),
        w_kv_up[0].astype(BF16), cf, sf, ch, shh)

    ya, yb = _mixers(qa, ka, va, qm, km, vb, lam_rows, subln_g[0][None, :])

    keys = peer_keys[0].reshape(2 * HP, N_KEYS, DK_HALF).astype(BF16)
    x1, h2, idx, g_t = _merge(
        ya.reshape(t, d), yb.reshape(t, d), gs.reshape(t, 2 * d), x.reshape(t, d),
        w_pa[0].astype(BF16), w_pb[0].astype(BF16), w_out[0].astype(BF16),
        gt1[:bsz], sc2[:bsz], sh2[:bsz], ln1_g[0][None, :], ln1_b[0][None, :],
        w_pq[0].astype(BF16), keys, n // TOKEN_TILE)

    w_t = _expert_down(idx, h2.reshape(t, 8, LANES), g_t, _pack_table(peer_u[0]))
    out = _expert_up(idx, w_t, x1.reshape(t, 8, LANES), _pack_table(peer_v[0]),
                     gt2[:bsz].reshape(bsz, 8, LANES), ln2_g[0].reshape(1, 8, LANES),
                     ln2_b[0].reshape(1, 8, LANES), n // EXPERT_TILE)
    return out.reshape(bsz, n, d)
```
